```python
import math
import jax, jax.numpy as jnp
from jax import lax
import numpy as np

D_MODEL = 2048
BATCH = 8
SEQ = 2048
DEPTH = 2
DEC_BATCH = 4
DEC_SEQ = 4096
PAST_LEN = 128

HEAD_DIM = 64
RET_HEADS = 8
RET_W = RET_HEADS * HEAD_DIM
RET_CHUNK = 128
NA_HEADS = 12
NA_W = NA_HEADS * HEAD_DIM
GRID_W = 64
NA_KH_MAX = 8
NA_KW = 16
NA_QB = 16
NA_HALO = NA_QB + NA_KW
DIFF_HEADS = 6
DIFF_QK_DIM = HEAD_DIM
DIFF_V_DIM = 2 * HEAD_DIM
DIFF_W = DIFF_HEADS * DIFF_V_DIM
DIFF_QBLOCK = 128
D_MIX = RET_W + NA_W + DIFF_W
D_IN = 4 * RET_W + 3 * NA_W + 3 * DIFF_W
D_FF = 5632
CONV_W = 3
NORM_EPS = 1e-6
NEG_INF = -1e30

kernel_name = 'hybrid_bidir_encoder'


def rms_norm(x, w):
    xf = x.astype(jnp.float32)
    y = xf * lax.rsqrt(jnp.mean(xf * xf, axis=-1, keepdims=True) + NORM_EPS)
    return (y * w.astype(jnp.float32)).astype(x.dtype)


def head_norm(x, w, center):
    xf = x.astype(jnp.float32)
    if center:
        xf = xf - jnp.mean(xf, axis=-1, keepdims=True)
    y = xf * lax.rsqrt(jnp.mean(xf * xf, axis=-1, keepdims=True) + NORM_EPS)
    return y.reshape(x.shape[:-2] + (-1,)) * w.astype(jnp.float32)


def retention_scan(q, k, v, log_gamma, include_diag):
    B, S, H, d = q.shape
    n = S // RET_CHUNK

    def chunks(t):
        return t.reshape(B, n, RET_CHUNK, H, d).transpose(1, 0, 3, 2, 4)

    pos = jnp.arange(RET_CHUNK, dtype=jnp.float32)
    delta = pos[:, None] - pos[None, :]
    lower = (delta >= 0) if include_diag else (delta > 0)
    lg = log_gamma[:, None, None]
    decay_in = jnp.where(lower[None], jnp.exp(lg * jnp.maximum(delta, 0.0)[None]), 0.0)
    q_decay = jnp.exp(log_gamma[:, None] * (pos + 1.0)[None])[..., None]
    k_decay = jnp.exp(log_gamma[:, None] * (RET_CHUNK - 1.0 - pos)[None])[..., None]
    chunk_decay = jnp.exp(log_gamma * RET_CHUNK)[:, None, None]

    def step(state, qkv):
        qc, kc, vc = qkv
        inner = jnp.einsum('bhid,bhjd->bhij', qc, kc) * decay_in
        out = (jnp.einsum('bhij,bhje->bhie', inner, vc)
               + jnp.einsum('bhid,bhde->bhie', qc * q_decay, state))
        state = state * chunk_decay + jnp.einsum('bhjd,bhje->bhde', kc * k_decay, vc)
        return state, out

    state0 = jnp.zeros((B, H, d, d), jnp.float32)
    _, out = lax.scan(step, state0, (chunks(q), chunks(k), chunks(v)))
    return out.transpose(1, 0, 3, 2, 4).reshape(B, S, H, d)


def retention_mixer(q, k, v, g, decay_fwd, decay_bwd, norm_w):
    B, S = q.shape[:2]
    qf = q.reshape(B, S, RET_HEADS, HEAD_DIM).astype(jnp.float32)
    kf = k.reshape(B, S, RET_HEADS, HEAD_DIM).astype(jnp.float32) * (HEAD_DIM ** -0.5)
    vf = v.reshape(B, S, RET_HEADS, HEAD_DIM).astype(jnp.float32)
    lg_f = jax.nn.log_sigmoid(decay_fwd.astype(jnp.float32))
    lg_b = jax.nn.log_sigmoid(decay_bwd.astype(jnp.float32))
    fwd = retention_scan(qf, kf, vf, lg_f, True)
    bwd = jnp.flip(retention_scan(jnp.flip(qf, 1), jnp.flip(kf, 1), jnp.flip(vf, 1), lg_b, False), 1)
    y = head_norm(fwd + bwd, norm_w, True)
    return (jax.nn.silu(g.astype(jnp.float32)) * y).astype(g.dtype)


def neighbourhood_attention(q, k, v, rpb):
    B, S = q.shape[:2]
    rows = S // GRID_W
    kh = min(NA_KH_MAX, rows)
    ncb = GRID_W // NA_QB
    qg = q.reshape(B, rows, ncb, NA_QB, NA_HEADS, HEAD_DIM)
    q_rows = jnp.moveaxis(qg, 1, 0)
    kg = k.reshape(B, rows, GRID_W, NA_HEADS, HEAD_DIM)
    vg = v.reshape(B, rows, GRID_W, NA_HEADS, HEAD_DIM)
    blk = np.arange(ncb)
    cb = np.clip(blk * NA_QB - NA_KW // 2, 0, GRID_W - NA_HALO)
    col_idx = cb[:, None] + np.arange(NA_HALO)[None, :]
    qcol = blk[:, None] * NA_QB + np.arange(NA_QB)[None, :]
    cs = np.clip(qcol - NA_KW // 2, 0, GRID_W - NA_KW)
    kcol = col_idx[:, None, :]
    in_win = (kcol >= cs[..., None]) & (kcol < cs[..., None] + NA_KW)
    dc_idx = np.clip(kcol - qcol[..., None] + NA_KW - 1, 0, 2 * NA_KW - 2)
    rpb_cols = rpb.astype(jnp.float32)[:, :, dc_idx]
    mask = jnp.asarray(in_win)[None, None, :, :, None, :]
    scale = HEAD_DIM ** -0.5

    def row(args):
        q_row, r = args
        rs = jnp.clip(r - kh // 2, 0, rows - kh)
        k_blk = lax.dynamic_slice_in_dim(kg, rs, kh, axis=1)[:, :, col_idx]
        v_blk = lax.dynamic_slice_in_dim(vg, rs, kh, axis=1)[:, :, col_idx]
        s = jnp.einsum('bnqhd,bknjhd->bhnqkj', q_row, k_blk).astype(jnp.float32) * scale
        dr_idx = rs + jnp.arange(kh) - r + NA_KH_MAX - 1
        bias = rpb_cols[:, dr_idx].transpose(0, 2, 3, 1, 4)
        s = jnp.where(mask, s + bias[None], NEG_INF)
        p = jax.nn.softmax(s.reshape(B, NA_HEADS, ncb, NA_QB, kh * NA_HALO), axis=-1)
        p = p.reshape(B, NA_HEADS, ncb, NA_QB, kh, NA_HALO).astype(v.dtype)
        return jnp.einsum('bhnqkj,bknjhd->bnqhd', p, v_blk)

    out = lax.map(row, (q_rows, jnp.arange(rows)))
    return jnp.moveaxis(out, 0, 1).reshape(B, S, NA_W)


def diff_attention(q, k, v, lam, lam_init, norm_w):
    B, S = q.shape[:2]
    qh = q.reshape(B, S, DIFF_HEADS, 2, DIFF_QK_DIM)
    kh = k.reshape(B, S, DIFF_HEADS, 2, DIFF_QK_DIM)
    vh = v.reshape(B, S, DIFF_HEADS, DIFF_V_DIM)
    scale = DIFF_QK_DIM ** -0.5
    slopes = 2.0 ** (-8.0 * (jnp.arange(DIFF_HEADS, dtype=jnp.float32) + 1.0) / DIFF_HEADS)
    nb = S // DIFF_QBLOCK
    q_blocks = qh.reshape(B, nb, DIFF_QBLOCK, DIFF_HEADS, 2, DIFF_QK_DIM).transpose(1, 0, 2, 3, 4, 5)
    kpos = jnp.arange(S, dtype=jnp.float32)

    def block(args):
        q_blk, start = args
        qpos = start.astype(jnp.float32) + jnp.arange(DIFF_QBLOCK, dtype=jnp.float32)
        bias = -slopes[:, None, None] * jnp.abs(qpos[:, None] - kpos[None, :])
        s = jnp.einsum('bqhcd,bkhcd->bhcqk', q_blk, kh).astype(jnp.float32) * scale + bias[None, :, None]
        p = jax.nn.softmax(s, axis=-1)
        a = p[:, :, 0] - lam * p[:, :, 1]
        return jnp.einsum('bhqk,bkhe->bqhe', a.astype(v.dtype), vh)

    out = lax.map(block, (q_blocks, jnp.arange(nb) * DIFF_QBLOCK))
    out = out.transpose(1, 0, 2, 3, 4).reshape(B, S, DIFF_HEADS, DIFF_V_DIM)
    return (head_norm(out, norm_w, False) * (1.0 - lam_init)).astype(v.dtype)


def conv_ffn(h, w_up, conv_w, conv_b, w_down):
    u = h @ w_up
    a, g = jnp.split(u, 2, axis=-1)
    gp = jnp.pad(g, ((0, 0), (1, 1), (0, 0)))
    g = gp[:, :-2] * conv_w[0] + gp[:, 1:-1] * conv_w[1] + gp[:, 2:] * conv_w[2] + conv_b
    return (jax.nn.gelu(g) * a) @ w_down


def encoder_trunk(x, norm1_w, w_in, ret_decay_fwd, ret_decay_bwd, ret_norm_w, na_rpb,
                  diff_lambda_q1, diff_lambda_k1, diff_lambda_q2, diff_lambda_k2, diff_norm_w,
                  w_out, norm2_w, ffn_w_up, ffn_conv_w, ffn_conv_b, ffn_w_down, final_norm_w):
    splits = list(np.cumsum([RET_W] * 4 + [NA_W] * 3 + [DIFF_W] * 3)[:-1])
    for l in range(DEPTH):
        h = rms_norm(x, norm1_w[l])
        proj = h @ w_in[l]
        rq, rk, rv, rg, nq, nk, nv, dq, dk, dv = jnp.split(proj, splits, axis=-1)
        ret_out = retention_mixer(rq, rk, rv, rg, ret_decay_fwd[l], ret_decay_bwd[l], ret_norm_w[l])
        na_out = neighbourhood_attention(nq, nk, nv, na_rpb[l])
        lam_init = 0.8 - 0.6 * math.exp(-0.3 * l)
        lam = (jnp.exp(jnp.sum(diff_lambda_q1[l].astype(jnp.float32) * diff_lambda_k1[l].astype(jnp.float32)))
               - jnp.exp(jnp.sum(diff_lambda_q2[l].astype(jnp.float32) * diff_lambda_k2[l].astype(jnp.float32)))
               + lam_init)
        diff_out = diff_attention(dq, dk, dv, lam, lam_init, diff_norm_w[l])
        mix = jnp.concatenate([ret_out, na_out.astype(x.dtype), diff_out], axis=-1)
        x = x + mix @ w_out[l]
        h = rms_norm(x, norm2_w[l])
        x = x + conv_ffn(h, ffn_w_up[l], ffn_conv_w[l], ffn_conv_b[l], ffn_w_down[l])
    return rms_norm(x, final_norm_w)


def setup_inputs(seed: int = 0) -> dict:
    key = jax.random.key(seed)
    ks = jax.random.split(key, 20)
    f32 = jnp.float32

    def nrm(k, shape, scale):
        return jax.random.normal(k, shape, f32) * scale

    def gain(k, shape):
        return 1.0 + nrm(k, shape, 0.01)

    decay_init = jnp.asarray(np.log(2.0 ** (5 + np.arange(RET_HEADS)) - 1.0), f32)
    return {
        'x_prompt': nrm(ks[0], (BATCH, SEQ, D_MODEL), 1.0),
        'x_sample': nrm(ks[1], (DEC_BATCH, DEC_SEQ, D_MODEL), 1.0),
        'norm1_w': gain(ks[2], (DEPTH, D_MODEL)),
        'w_in': nrm(ks[3], (DEPTH, D_MODEL, D_IN), D_MODEL ** -0.5),
        'ret_decay_fwd': decay_init + nrm(ks[4], (DEPTH, RET_HEADS), 0.01),
        'ret_decay_bwd': decay_init + nrm(ks[5], (DEPTH, RET_HEADS), 0.01),
        'ret_norm_w': gain(ks[6], (DEPTH, RET_W)),
        'na_rpb': nrm(ks[7], (DEPTH, NA_HEADS, 2 * NA_KH_MAX - 1, 2 * NA_KW - 1), 0.02),
        'diff_lambda_q1': nrm(ks[8], (DEPTH, DIFF_QK_DIM), 0.1),
        'diff_lambda_k1': nrm(ks[9], (DEPTH, DIFF_QK_DIM), 0.1),
        'diff_lambda_q2': nrm(ks[10], (DEPTH, DIFF_QK_DIM), 0.1),
        'diff_lambda_k2': nrm(ks[11], (DEPTH, DIFF_QK_DIM), 0.1),
        'diff_norm_w': gain(ks[12], (DEPTH, DIFF_W)),
        'w_out': nrm(ks[13], (DEPTH, D_MIX, D_MODEL), D_MIX ** -0.5),
        'norm2_w': gain(ks[14], (DEPTH, D_MODEL)),
        'ffn_w_up': nrm(ks[15], (DEPTH, D_MODEL, 2 * D_FF), D_MODEL ** -0.5),
        'ffn_conv_w': nrm(ks[16], (DEPTH, CONV_W, D_FF), CONV_W ** -0.5),
        'ffn_conv_b': nrm(ks[17], (DEPTH, D_FF), 0.01),
        'ffn_w_down': nrm(ks[18], (DEPTH, D_FF, D_MODEL), D_FF ** -0.5),
        'final_norm_w': gain(ks[19], (D_MODEL,)),
    }


def reference(x_prompt, x_sample, norm1_w, w_in, ret_decay_fwd, ret_decay_bwd, ret_norm_w, na_rpb,
              diff_lambda_q1, diff_lambda_k1, diff_lambda_q2, diff_lambda_k2, diff_norm_w,
              w_out, norm2_w, ffn_w_up, ffn_conv_w, ffn_conv_b, ffn_w_down, final_norm_w):
    y_prompt = encoder_trunk(x_prompt, norm1_w, w_in, ret_decay_fwd, ret_decay_bwd, ret_norm_w, na_rpb,
                             diff_lambda_q1, diff_lambda_k1, diff_lambda_q2, diff_lambda_k2, diff_norm_w,
                             w_out, norm2_w, ffn_w_up, ffn_conv_w, ffn_conv_b, ffn_w_down, final_norm_w)
    y_sample = encoder_trunk(x_sample, norm1_w, w_in, ret_decay_fwd, ret_decay_bwd, ret_norm_w, na_rpb,
                             diff_lambda_q1, diff_lambda_k1, diff_lambda_q2, diff_lambda_k2, diff_norm_w,
                             w_out, norm2_w, ffn_w_up, ffn_conv_w, ffn_conv_b, ffn_w_down, final_norm_w)
    return (y_prompt, y_sample)
```

```python
import functools
import math

import jax
import jax.numpy as jnp
import numpy as np
from jax import lax
from jax.experimental import pallas as pl
from jax.experimental.pallas import tpu as pltpu

D_MODEL = 2048
DEPTH = 2
HEAD_DIM = 64
RET_HEADS = 8
RET_W = RET_HEADS * HEAD_DIM
NA_HEADS = 12
NA_W = NA_HEADS * HEAD_DIM
GRID_W = 64
NA_KH = 8
NA_KW = 16
DIFF_HEADS = 6
DIFF_V_DIM = 2 * HEAD_DIM
DIFF_W = DIFF_HEADS * DIFF_V_DIM
D_IN = 4 * RET_W + 3 * NA_W + 3 * DIFF_W
D_FF = 5632
NORM_EPS = 1e-6
NEG_INF = -1e30

LANES = 128
BF16_SUBLANES = 16
RET_CHUNK = 256
VMEM_LIMIT = 56 * 1024 * 1024

NA_COL0 = 0
DIFF_COL0 = 3 * NA_W
RET_COL0 = 3 * NA_W + 3 * DIFF_W

f32 = jnp.float32
bf16 = jnp.bfloat16


def _nt_dot(a, b):
    return lax.dot_general(a, b, (((1,), (1,)), ((), ())), preferred_element_type=f32)


def _tn_dot(a, b):
    return lax.dot_general(a, b, (((0,), (0,)), ((), ())), preferred_element_type=f32)


def _dot(a, b):
    return jnp.dot(a, b, preferred_element_type=f32)


def _norm_matmul_kernel(x_ref, nw_ref, w_ref, o_ref, h_ref):
    @pl.when(pl.program_id(1) == 0)
    def _():
        x = x_ref[...]
        y = x * lax.rsqrt(jnp.mean(x * x, axis=-1, keepdims=True) + NORM_EPS)
        h_ref[...] = (y * nw_ref[...]).astype(bf16)

    o_ref[...] = _dot(h_ref[...], w_ref[...]).astype(o_ref.dtype)


def norm_matmul(x, nw, w, *, tm=1024, tn=512):
    T, D = x.shape
    N = w.shape[1]
    return pl.pallas_call(
        _norm_matmul_kernel,
        grid=(T // tm, N // tn),
        in_specs=[
            pl.BlockSpec((tm, D), lambda i, j: (i, 0)),
            pl.BlockSpec((1, D), lambda i, j: (0, 0)),
            pl.BlockSpec((D, tn), lambda i, j: (0, j)),
        ],
        out_specs=pl.BlockSpec((tm, tn), lambda i, j: (i, j)),
        out_shape=jax.ShapeDtypeStruct((T, N), bf16),
        scratch_shapes=[pltpu.VMEM((tm, D), bf16)],
        compiler_params=pltpu.CompilerParams(
            dimension_semantics=("parallel", "arbitrary"), vmem_limit_bytes=VMEM_LIMIT),
        name="norm_matmul",
    )(x, nw.reshape(1, D), w)


def _retention_kernel(lgf_ref, lgb_ref, q_ref, k_ref, v_ref, g_ref, nw_ref, o_ref, acc_ref, *, S):
    C = RET_CHUNK
    n = S // C
    pair = pl.program_id(1)
    scale = HEAD_DIM ** -0.5

    lane = lax.broadcasted_iota(jnp.int32, (1, LANES), 1)
    lo = lane < HEAD_DIM
    lgf = jnp.where(lo, lgf_ref[2 * pair], lgf_ref[2 * pair + 1])
    lgb = jnp.where(lo, lgb_ref[2 * pair], lgb_ref[2 * pair + 1])

    pos = lax.broadcasted_iota(jnp.int32, (C, 1), 0).astype(f32)
    qdec_f = jnp.exp(lgf * (pos + 1.0))
    kdec_f = jnp.exp(lgf * (C - 1.0 - pos)) * scale
    cdec_f = jnp.exp(lgf * float(C))
    qdec_b = jnp.exp(lgb * (C - pos))
    kdec_b = jnp.exp(lgb * pos) * scale
    cdec_b = jnp.exp(lgb * float(C))

    ri = lax.broadcasted_iota(jnp.int32, (C, C), 0)
    ci = lax.broadcasted_iota(jnp.int32, (C, C), 1)
    delta = (ri - ci).astype(f32)

    def decay_mat(h):
        fwd = jnp.exp(lgf_ref[2 * pair + h] * jnp.maximum(delta, 0.0))
        bwd = jnp.exp(lgb_ref[2 * pair + h] * jnp.maximum(-delta, 0.0))
        return jnp.where(delta >= 0, fwd, bwd) * scale

    dmat0 = decay_mat(0)
    dmat1 = decay_mat(1)

    r2 = lax.broadcasted_iota(jnp.int32, (LANES, LANES), 0) // HEAD_DIM
    c2 = lax.broadcasted_iota(jnp.int32, (LANES, LANES), 1) // HEAD_DIM
    same_head = r2 == c2
    avg = jnp.where(same_head, 1.0 / HEAD_DIM, 0.0).astype(bf16)

    def chunk(ref, c):
        return ref[pl.ds(pl.multiple_of(c * C, C), C), :]

    def fwd_body(c, state):
        qc, kc, vc = chunk(q_ref, c), chunk(k_ref, c), chunk(v_ref, c)
        zero = jnp.zeros_like(qc)
        in0 = _nt_dot(jnp.where(lo, qc, zero), kc)
        in1 = _nt_dot(jnp.where(lo, zero, qc), kc)
        w = jnp.concatenate([(in0 * dmat0).astype(bf16), (in1 * dmat1).astype(bf16)], axis=1)
        vv = jnp.concatenate([jnp.where(lo, vc, zero), jnp.where(lo, zero, vc)], axis=0)
        intra = _dot(w, vv)
        qd = (qc.astype(f32) * qdec_f).astype(bf16)
        cross = _dot(qd, state.astype(bf16))
        acc_ref[pl.ds(pl.multiple_of(c * C, C), C), :] = intra + cross
        kd = (kc.astype(f32) * kdec_f).astype(bf16)
        upd = _tn_dot(kd, vc)
        return state * cdec_f + jnp.where(same_head, upd, 0.0)

    lax.fori_loop(0, n, fwd_body, jnp.zeros((LANES, LANES), f32))

    def lane_mean(y):
        hi = y.astype(bf16)
        lo_part = (y - hi.astype(f32)).astype(bf16)
        return _dot(hi, avg) + _dot(lo_part, avg)

    def bwd_body(t, state):
        c = n - 1 - t
        qc, kc, vc = chunk(q_ref, c), chunk(k_ref, c), chunk(v_ref, c)
        qd = (qc.astype(f32) * qdec_b).astype(bf16)
        y = chunk(acc_ref, c) + _dot(qd, state.astype(bf16))
        y = y - lane_mean(y)
        y = y * lax.rsqrt(lane_mean(y * y) + NORM_EPS)
        gc = chunk(g_ref, c).astype(f32)
        out = jax.nn.silu(gc) * (y * nw_ref[...])
        o_ref[pl.ds(pl.multiple_of(c * C, C), C), :] = out.astype(o_ref.dtype)
        kd = (kc.astype(f32) * kdec_b).astype(bf16)
        upd = _tn_dot(kd, vc)
        return state * cdec_b + jnp.where(same_head, upd, 0.0)

    lax.fori_loop(0, n, bwd_body, jnp.zeros((LANES, LANES), f32))


def retention(proj, lgf, lgb, norm_w, *, B, S):
    T = B * S
    blk0 = RET_COL0 // LANES
    npair = RET_HEADS // 2

    def col(seg):
        return lambda b, p: (b, blk0 + seg * npair + p)

    smem = pl.BlockSpec(memory_space=pltpu.SMEM)
    return pl.pallas_call(
        functools.partial(_retention_kernel, S=S),
        grid=(B, npair),
        in_specs=[smem, smem,
                  pl.BlockSpec((S, LANES), col(0)),
                  pl.BlockSpec((S, LANES), col(1)),
                  pl.BlockSpec((S, LANES), col(2)),
                  pl.BlockSpec((S, LANES), col(3)),
                  pl.BlockSpec((1, LANES), lambda b, p: (0, p))],
        out_specs=pl.BlockSpec((S, LANES), lambda b, p: (b, p)),
        out_shape=jax.ShapeDtypeStruct((T, RET_W), bf16),
        scratch_shapes=[pltpu.VMEM((S, LANES), f32)],
        compiler_params=pltpu.CompilerParams(
            dimension_semantics=("parallel", "parallel"), vmem_limit_bytes=VMEM_LIMIT),
        name="retention",
    )(lgf, lgb, proj, proj, proj, proj, norm_w.reshape(1, RET_W))


def _na_window_start(r, rows):
    return jnp.clip(r - NA_KH // 2, 0, rows - NA_KH)


def _na_kernel(q_ref, k_ref, v_ref, bias_ref, o_ref, *, rows):
    r = pl.program_id(1)
    nwin = NA_KH * GRID_W
    start = pl.multiple_of(_na_window_start(r, rows) * GRID_W, GRID_W)
    lane = lax.broadcasted_iota(jnp.int32, (1, LANES), 1)
    lo = lane < HEAD_DIM
    scale = HEAD_DIM ** -0.5
    for p in range(NA_HEADS // 2):
        cols = slice(p * LANES, (p + 1) * LANES)
        qp = q_ref[:, cols]
        kp = k_ref[pl.ds(start, nwin), cols]
        vp = v_ref[pl.ds(start, nwin), cols]
        out = jnp.zeros((GRID_W, LANES), f32)
        for half in range(2):
            keep = lo if half == 0 else jnp.logical_not(lo)
            s = _nt_dot(jnp.where(keep, qp, jnp.zeros_like(qp)), kp) * scale + bias_ref[2 * p + half]
            e = jnp.exp(s - jnp.max(s, axis=-1, keepdims=True))
            prob = e / jnp.sum(e, axis=-1, keepdims=True)
            out = out + _dot(prob.astype(bf16), jnp.where(keep, vp, jnp.zeros_like(vp)))
        o_ref[:, cols] = out.astype(o_ref.dtype)


def neighbourhood_attention(proj, bias_table, *, B, S):
    T = B * S
    rows = S // GRID_W
    nwin = NA_KH * GRID_W

    def cfg(b, r):
        return (r - _na_window_start(r, rows), 0, 0, 0)

    return pl.pallas_call(
        functools.partial(_na_kernel, rows=rows),
        grid=(B, rows),
        in_specs=[pl.BlockSpec((GRID_W, NA_W), lambda b, r: (b * rows + r, 0)),
                  pl.BlockSpec((S, NA_W), lambda b, r: (b, 1)),
                  pl.BlockSpec((S, NA_W), lambda b, r: (b, 2)),
                  pl.BlockSpec((None, NA_HEADS, GRID_W, nwin), cfg)],
        out_specs=pl.BlockSpec((GRID_W, NA_W), lambda b, r: (b * rows + r, 0)),
        out_shape=jax.ShapeDtypeStruct((T, NA_W), bf16),
        compiler_params=pltpu.CompilerParams(
            dimension_semantics=("parallel", "arbitrary"), vmem_limit_bytes=VMEM_LIMIT),
        name="neighbourhood_attention",
    )(proj, proj, proj, bias_table)


def na_bias_table(rpb):
    qcol = np.arange(GRID_W)[:, None]
    kcol = np.arange(GRID_W)[None, :]
    cs = np.clip(qcol - NA_KW // 2, 0, GRID_W - NA_KW)
    in_win = (kcol >= cs) & (kcol < cs + NA_KW)
    dc = np.clip(kcol - qcol + NA_KW - 1, 0, 2 * NA_KW - 2)
    cfgs = np.arange(NA_KH)[:, None]
    dr = np.arange(NA_KH)[None, :] - cfgs + NA_KH - 1
    tab = rpb.astype(f32)[:, dr][:, :, :, dc]
    tab = jnp.where(jnp.asarray(in_win)[None, None, None], tab, NEG_INF)
    tab = tab.transpose(1, 0, 3, 2, 4)
    return tab.reshape(NA_KH, NA_HEADS, GRID_W, NA_KH * GRID_W)


def _diff_kernel(slope_ref, lam_ref, q_ref, k_ref, v_ref, nw_ref, o_ref, *, S, tq, out_scale):
    h = pl.program_id(1)
    qi = pl.program_id(2)
    lane = lax.broadcasted_iota(jnp.int32, (1, LANES), 1)
    lo = lane < HEAD_DIM
    q = q_ref[...]
    k = k_ref[...]
    zero = jnp.zeros_like(q)
    scale = HEAD_DIM ** -0.5
    qpos = (qi * tq + lax.broadcasted_iota(jnp.int32, (tq, 1), 0)).astype(f32)
    kpos = lax.broadcasted_iota(jnp.int32, (1, S), 1).astype(f32)
    bias = -slope_ref[h] * jnp.abs(qpos - kpos)

    def softmax(qm):
        s = _nt_dot(qm, k) * scale + bias
        e = jnp.exp(s - jnp.max(s, axis=-1, keepdims=True))
        return e / jnp.sum(e, axis=-1, keepdims=True)

    a = softmax(jnp.where(lo, q, zero)) - lam_ref[0] * softmax(jnp.where(lo, zero, q))
    out = _dot(a.astype(bf16), v_ref[...])
    y = out * lax.rsqrt(jnp.mean(out * out, axis=-1, keepdims=True) + NORM_EPS)
    o_ref[...] = (y * nw_ref[...] * out_scale).astype(o_ref.dtype)


def diff_attention(proj, slopes, lam, norm_w, *, B, S, lam_init):
    T = B * S
    tq = 512 if S <= 2048 else 256
    nq = S // tq
    blk0 = DIFF_COL0 // LANES
    smem = pl.BlockSpec(memory_space=pltpu.SMEM)
    return pl.pallas_call(
        functools.partial(_diff_kernel, S=S, tq=tq, out_scale=1.0 - lam_init),
        grid=(B, DIFF_HEADS, nq),
        in_specs=[smem, smem,
                  pl.BlockSpec((tq, LANES), lambda b, h, i: (b * nq + i, blk0 + h)),
                  pl.BlockSpec((S, LANES), lambda b, h, i: (b, blk0 + DIFF_HEADS + h)),
                  pl.BlockSpec((S, LANES), lambda b, h, i: (b, blk0 + 2 * DIFF_HEADS + h)),
                  pl.BlockSpec((1, LANES), lambda b, h, i: (0, h))],
        out_specs=pl.BlockSpec((tq, LANES), lambda b, h, i: (b * nq + i, h)),
        out_shape=jax.ShapeDtypeStruct((T, DIFF_W), bf16),
        compiler_params=pltpu.CompilerParams(
            dimension_semantics=("parallel", "parallel", "arbitrary"), vmem_limit_bytes=VMEM_LIMIT),
        name="diff_attention",
    )(slopes, lam, proj, proj, proj, norm_w.reshape(1, DIFF_W))


def _out_proj_kernel(x_ref, r_ref, n_ref, d_ref, w_ref, o_ref):
    acc = _dot(r_ref[...], w_ref[0:RET_W, :])
    acc += _dot(n_ref[...], w_ref[RET_W:RET_W + NA_W, :])
    acc += _dot(d_ref[...], w_ref[RET_W + NA_W:, :])
    o_ref[...] = x_ref[...] + acc


def out_proj(x, ret, na, diff, w_out, *, tm=512):
    T, D = x.shape
    row = lambda i: (i, 0)
    return pl.pallas_call(
        _out_proj_kernel,
        grid=(T // tm,),
        in_specs=[pl.BlockSpec((tm, D), row),
                  pl.BlockSpec((tm, RET_W), row),
                  pl.BlockSpec((tm, NA_W), row),
                  pl.BlockSpec((tm, DIFF_W), row),
                  pl.BlockSpec((D, D), lambda i: (0, 0))],
        out_specs=pl.BlockSpec((tm, D), row),
        out_shape=jax.ShapeDtypeStruct((T, D), f32),
        compiler_params=pltpu.CompilerParams(
            dimension_semantics=("parallel",), vmem_limit_bytes=VMEM_LIMIT),
        name="out_proj",
    )(x, ret, na, diff, w_out)


def _ffn_down_kernel(x_ref, a_ref, g_ref, gp_ref, gn_ref, cw_ref, cb_ref, wd_ref, fw_ref, o_ref,
                     *, S, tm, final_norm):
    i = pl.program_id(0)
    k = pl.program_id(1)

    @pl.when(k == 0)
    def _():
        o_ref[...] = x_ref[...]

    g = g_ref[...].astype(f32)
    row = lax.broadcasted_iota(jnp.int32, (tm, 1), 0)
    t0 = i * tm
    at_seq_start = (t0 % S) == 0
    at_seq_end = ((t0 + tm) % S) == 0
    prev = jnp.where(at_seq_start, 0.0, gp_ref[BF16_SUBLANES - 1:BF16_SUBLANES, :].astype(f32))
    nxt = jnp.where(at_seq_end, 0.0, gn_ref[0:1, :].astype(f32))
    g_up = jnp.where(row == 0, prev, pltpu.roll(g, 1, 0))
    g_dn = jnp.where(row == tm - 1, nxt, pltpu.roll(g, tm - 1, 0))
    c = g_up * cw_ref[0:1, :] + g * cw_ref[1:2, :] + g_dn * cw_ref[2:3, :] + cb_ref[...]
    gated = (jax.nn.gelu(c) * a_ref[...].astype(f32)).astype(bf16)
    o_ref[...] += _dot(gated, wd_ref[...])

    if final_norm:
        @pl.when(k == pl.num_programs(1) - 1)
        def _():
            x = o_ref[...]
            y = x * lax.rsqrt(jnp.mean(x * x, axis=-1, keepdims=True) + NORM_EPS)
            o_ref[...] = y * fw_ref[...]


def ffn_down(x, u, conv_w, conv_b, w_down, final_w, *, S, final_norm, tm=512, tk=512):
    T, D = x.shape
    nk = D_FF // tk
    rb = tm // BF16_SUBLANES
    last_rb = T // BF16_SUBLANES - 1
    return pl.pallas_call(
        functools.partial(_ffn_down_kernel, S=S, tm=tm, final_norm=final_norm),
        grid=(T // tm, nk),
        in_specs=[pl.BlockSpec((tm, D), lambda i, k: (i, 0)),
                  pl.BlockSpec((tm, tk), lambda i, k: (i, k)),
                  pl.BlockSpec((tm, tk), lambda i, k: (i, nk + k)),
                  pl.BlockSpec((BF16_SUBLANES, tk), lambda i, k: (jnp.maximum(i * rb - 1, 0), nk + k)),
                  pl.BlockSpec((BF16_SUBLANES, tk), lambda i, k: (jnp.minimum((i + 1) * rb, last_rb), nk + k)),
                  pl.BlockSpec((3, tk), lambda i, k: (0, k)),
                  pl.BlockSpec((1, tk), lambda i, k: (0, k)),
                  pl.BlockSpec((tk, D), lambda i, k: (k, 0)),
                  pl.BlockSpec((1, D), lambda i, k: (0, 0))],
        out_specs=pl.BlockSpec((tm, D), lambda i, k: (i, 0)),
        out_shape=jax.ShapeDtypeStruct((T, D), f32),
        compiler_params=pltpu.CompilerParams(
            dimension_semantics=("parallel", "arbitrary"), vmem_limit_bytes=VMEM_LIMIT),
        name="ffn_down",
    )(x, u, u, u, u, conv_w, conv_b.reshape(1, D_FF), w_down, final_w.reshape(1, D))


def _prepare_layer(l, norm1_w, w_in, ret_decay_fwd, ret_decay_bwd, ret_norm_w, na_rpb,
                   lq1, lk1, lq2, lk2, diff_norm_w, w_out, norm2_w, ffn_w_up, ffn_conv_w,
                   ffn_conv_b, ffn_w_down):
    r0, n0, d0 = 0, 4 * RET_W, 4 * RET_W + 3 * NA_W
    w = w_in[l]
    w_perm = jnp.concatenate([w[:, n0:d0], w[:, d0:], w[:, r0:n0]], axis=1).astype(bf16)
    lam_init = 0.8 - 0.6 * math.exp(-0.3 * l)
    lam = (jnp.exp(jnp.sum(lq1[l].astype(f32) * lk1[l].astype(f32)))
           - jnp.exp(jnp.sum(lq2[l].astype(f32) * lk2[l].astype(f32))) + lam_init)
    return dict(
        norm1_w=norm1_w[l].astype(f32), w_in=w_perm,
        lgf=jax.nn.log_sigmoid(ret_decay_fwd[l].astype(f32)),
        lgb=jax.nn.log_sigmoid(ret_decay_bwd[l].astype(f32)),
        ret_norm_w=ret_norm_w[l].astype(f32),
        na_bias=na_bias_table(na_rpb[l]),
        lam=lam.reshape(1), lam_init=lam_init, diff_norm_w=diff_norm_w[l].astype(f32),
        w_out=w_out[l].astype(bf16), norm2_w=norm2_w[l].astype(f32),
        w_up=ffn_w_up[l].astype(bf16), conv_w=ffn_conv_w[l].astype(f32),
        conv_b=ffn_conv_b[l].astype(f32), w_down=ffn_w_down[l].astype(bf16))


def _trunk(x, layers, final_norm_w, slopes):
    B, S, D = x.shape
    x = x.reshape(B * S, D)
    for l, p in enumerate(layers):
        proj = norm_matmul(x, p["norm1_w"], p["w_in"])
        ret = retention(proj, p["lgf"], p["lgb"], p["ret_norm_w"], B=B, S=S)
        na = neighbourhood_attention(proj, p["na_bias"], B=B, S=S)
        diff = diff_attention(proj, slopes, p["lam"], p["diff_norm_w"], B=B, S=S, lam_init=p["lam_init"])
        x = out_proj(x, ret, na, diff, p["w_out"])
        u = norm_matmul(x, p["norm2_w"], p["w_up"])
        x = ffn_down(x, u, p["conv_w"], p["conv_b"], p["w_down"], final_norm_w.astype(f32),
                     S=S, final_norm=(l == len(layers) - 1))
    return x.reshape(B, S, D)


def kernel(x_prompt, x_sample, norm1_w, w_in, ret_decay_fwd, ret_decay_bwd, ret_norm_w, na_rpb, diff_lambda_q1, diff_lambda_k1, diff_lambda_q2, diff_lambda_k2, diff_norm_w, w_out, norm2_w, ffn_w_up, ffn_conv_w, ffn_conv_b, ffn_w_down, final_norm_w):
    layers = [_prepare_layer(l, norm1_w, w_in, ret_decay_fwd, ret_decay_bwd, ret_norm_w, na_rpb,
                             diff_lambda_q1, diff_lambda_k1, diff_lambda_q2, diff_lambda_k2,
                             diff_norm_w, w_out, norm2_w, ffn_w_up, ffn_conv_w, ffn_conv_b, ffn_w_down)
              for l in range(w_in.shape[0])]
    slopes = 2.0 ** (-8.0 * (jnp.arange(DIFF_HEADS, dtype=f32) + 1.0) / DIFF_HEADS)
    y_prompt = _trunk(x_prompt, layers, final_norm_w, slopes)
    y_sample = _trunk(x_sample, layers, final_norm_w, slopes)
    return (y_prompt, y_sample)
```

```python
import functools
import math

import jax
import jax.numpy as jnp
import numpy as np
from jax import lax
from jax.experimental import pallas as pl
from jax.experimental.pallas import tpu as pltpu

D_MODEL = 2048
DEPTH = 2
HEAD_DIM = 64
RET_HEADS = 8
RET_W = RET_HEADS * HEAD_DIM
NA_HEADS = 12
NA_W = NA_HEADS * HEAD_DIM
GRID_W = 64
NA_KH = 8
NA_KW = 16
DIFF_HEADS = 6
DIFF_V_DIM = 2 * HEAD_DIM
DIFF_W = DIFF_HEADS * DIFF_V_DIM
D_IN = 4 * RET_W + 3 * NA_W + 3 * DIFF_W
D_FF = 5632
NORM_EPS = 1e-6
NEG_INF = -1e30

LANES = 128
BF16_SUBLANES = 16
RET_CHUNK = 256
NA_ROWS_PER_STEP = 4
DIFF_TQ = 256
DIFF_POS_RADIX = 64
DIFF_SLOPE_PARTS = 3
LOG2E = 1.4426950408889634
VMEM_LIMIT = 56 * 1024 * 1024

NA_COL0 = 0
DIFF_COL0 = 3 * NA_W
RET_COL0 = 3 * NA_W + 3 * DIFF_W

f32 = jnp.float32
bf16 = jnp.bfloat16


def _nt_dot(a, b):
    return lax.dot_general(a, b, (((1,), (1,)), ((), ())), preferred_element_type=f32)


def _tn_dot(a, b):
    return lax.dot_general(a, b, (((0,), (0,)), ((), ())), preferred_element_type=f32)


def _dot(a, b):
    return jnp.dot(a, b, preferred_element_type=f32)


def _norm_matmul_kernel(x_ref, nw_ref, w_ref, o_ref, h_ref):
    @pl.when(pl.program_id(1) == 0)
    def _():
        x = x_ref[...]
        y = x * lax.rsqrt(jnp.mean(x * x, axis=-1, keepdims=True) + NORM_EPS)
        h_ref[...] = (y * nw_ref[...]).astype(bf16)

    o_ref[...] = _dot(h_ref[...], w_ref[...]).astype(o_ref.dtype)


def norm_matmul(x, nw, w, *, tm=1024, tn=512):
    T, D = x.shape
    N = w.shape[1]
    return pl.pallas_call(
        _norm_matmul_kernel,
        grid=(T // tm, N // tn),
        in_specs=[
            pl.BlockSpec((tm, D), lambda i, j: (i, 0)),
            pl.BlockSpec((1, D), lambda i, j: (0, 0)),
            pl.BlockSpec((D, tn), lambda i, j: (0, j)),
        ],
        out_specs=pl.BlockSpec((tm, tn), lambda i, j: (i, j)),
        out_shape=jax.ShapeDtypeStruct((T, N), bf16),
        scratch_shapes=[pltpu.VMEM((tm, D), bf16)],
        compiler_params=pltpu.CompilerParams(
            dimension_semantics=("parallel", "arbitrary"), vmem_limit_bytes=VMEM_LIMIT),
        name="norm_matmul",
    )(x, nw.reshape(1, D), w)


def _retention_kernel(lgf_ref, lgb_ref, q_ref, k_ref, v_ref, g_ref, nw_ref, o_ref, acc_ref, *, S):
    C = RET_CHUNK
    n = S // C
    pair = pl.program_id(1)
    scale = HEAD_DIM ** -0.5

    lane = lax.broadcasted_iota(jnp.int32, (1, LANES), 1)
    lo = lane < HEAD_DIM
    lgf = jnp.where(lo, lgf_ref[2 * pair], lgf_ref[2 * pair + 1])
    lgb = jnp.where(lo, lgb_ref[2 * pair], lgb_ref[2 * pair + 1])

    pos = lax.broadcasted_iota(jnp.int32, (C, 1), 0).astype(f32)
    qdec_f = jnp.exp(lgf * (pos + 1.0))
    kdec_f = jnp.exp(lgf * (C - 1.0 - pos)) * scale
    cdec_f = jnp.exp(lgf * float(C))
    qdec_b = jnp.exp(lgb * (C - pos))
    kdec_b = jnp.exp(lgb * pos) * scale
    cdec_b = jnp.exp(lgb * float(C))

    ri = lax.broadcasted_iota(jnp.int32, (C, C), 0)
    ci = lax.broadcasted_iota(jnp.int32, (C, C), 1)
    delta = (ri - ci).astype(f32)

    def decay_mat(h):
        fwd = jnp.exp(lgf_ref[2 * pair + h] * jnp.maximum(delta, 0.0))
        bwd = jnp.exp(lgb_ref[2 * pair + h] * jnp.maximum(-delta, 0.0))
        return jnp.where(delta >= 0, fwd, bwd) * scale

    dmat0 = decay_mat(0)
    dmat1 = decay_mat(1)

    r2 = lax.broadcasted_iota(jnp.int32, (LANES, LANES), 0) // HEAD_DIM
    c2 = lax.broadcasted_iota(jnp.int32, (LANES, LANES), 1) // HEAD_DIM
    same_head = r2 == c2
    avg = jnp.where(same_head, 1.0 / HEAD_DIM, 0.0).astype(bf16)

    def chunk(ref, c):
        return ref[pl.ds(pl.multiple_of(c * C, C), C), :]

    def fwd_body(c, state):
        qc, kc, vc = chunk(q_ref, c), chunk(k_ref, c), chunk(v_ref, c)
        zero = jnp.zeros_like(qc)
        in0 = _nt_dot(jnp.where(lo, qc, zero), kc)
        in1 = _nt_dot(jnp.where(lo, zero, qc), kc)
        w = jnp.concatenate([(in0 * dmat0).astype(bf16), (in1 * dmat1).astype(bf16)], axis=1)
        vv = jnp.concatenate([jnp.where(lo, vc, zero), jnp.where(lo, zero, vc)], axis=0)
        intra = _dot(w, vv)
        qd = (qc.astype(f32) * qdec_f).astype(bf16)
        cross = _dot(qd, state.astype(bf16))
        acc_ref[pl.ds(pl.multiple_of(c * C, C), C), :] = intra + cross
        kd = (kc.astype(f32) * kdec_f).astype(bf16)
        upd = _tn_dot(kd, vc)
        return state * cdec_f + jnp.where(same_head, upd, 0.0)

    lax.fori_loop(0, n, fwd_body, jnp.zeros((LANES, LANES), f32))

    def lane_mean(y):
        hi = y.astype(bf16)
        lo_part = (y - hi.astype(f32)).astype(bf16)
        return _dot(hi, avg) + _dot(lo_part, avg)

    def bwd_body(t, state):
        c = n - 1 - t
        qc, kc, vc = chunk(q_ref, c), chunk(k_ref, c), chunk(v_ref, c)
        qd = (qc.astype(f32) * qdec_b).astype(bf16)
        y = chunk(acc_ref, c) + _dot(qd, state.astype(bf16))
        y = y - lane_mean(y)
        y = y * lax.rsqrt(lane_mean(y * y) + NORM_EPS)
        gc = chunk(g_ref, c).astype(f32)
        out = jax.nn.silu(gc) * (y * nw_ref[...])
        o_ref[pl.ds(pl.multiple_of(c * C, C), C), :] = out.astype(o_ref.dtype)
        kd = (kc.astype(f32) * kdec_b).astype(bf16)
        upd = _tn_dot(kd, vc)
        return state * cdec_b + jnp.where(same_head, upd, 0.0)

    lax.fori_loop(0, n, bwd_body, jnp.zeros((LANES, LANES), f32))


def retention(proj, lgf, lgb, norm_w, *, B, S):
    T = B * S
    blk0 = RET_COL0 // LANES
    npair = RET_HEADS // 2

    def col(seg):
        return lambda b, p: (b, blk0 + seg * npair + p)

    smem = pl.BlockSpec(memory_space=pltpu.SMEM)
    return pl.pallas_call(
        functools.partial(_retention_kernel, S=S),
        grid=(B, npair),
        in_specs=[smem, smem,
                  pl.BlockSpec((S, LANES), col(0)),
                  pl.BlockSpec((S, LANES), col(1)),
                  pl.BlockSpec((S, LANES), col(2)),
                  pl.BlockSpec((S, LANES), col(3)),
                  pl.BlockSpec((1, LANES), lambda b, p: (0, p))],
        out_specs=pl.BlockSpec((S, LANES), lambda b, p: (b, p)),
        out_shape=jax.ShapeDtypeStruct((T, RET_W), bf16),
        scratch_shapes=[pltpu.VMEM((S, LANES), f32)],
        compiler_params=pltpu.CompilerParams(
            dimension_semantics=("parallel", "parallel"), vmem_limit_bytes=VMEM_LIMIT),
        name="retention",
    )(lgf, lgb, proj, proj, proj, proj, norm_w.reshape(1, RET_W))


def _na_window_start(r, rows):
    return jnp.clip(r - NA_KH // 2, 0, rows - NA_KH)


def _na_kernel(q_ref, k_ref, v_ref, bias_ref, o_ref, *, rows):
    g = pl.program_id(1)
    nwin = NA_KH * GRID_W
    lane = lax.broadcasted_iota(jnp.int32, (1, LANES), 1)
    lo = lane < HEAD_DIM
    scale = HEAD_DIM ** -0.5
    npair = NA_HEADS // 2
    for j in range(NA_ROWS_PER_STEP):
        r = g * NA_ROWS_PER_STEP + j
        start = pl.multiple_of(_na_window_start(r, rows) * GRID_W, GRID_W)
        tok = slice(j * GRID_W, (j + 1) * GRID_W)
        parts = []
        for p in range(npair):
            cols = slice(p * LANES, (p + 1) * LANES)
            qp = (q_ref[tok, cols].astype(f32) * scale).astype(bf16)
            kp = k_ref[pl.ds(start, nwin), cols]
            zero = jnp.zeros_like(qp)
            parts.append(_nt_dot(jnp.where(lo, qp, zero), kp))
            parts.append(_nt_dot(jnp.where(lo, zero, qp), kp))
        s = jnp.concatenate(parts, axis=0) + bias_ref[j].reshape(NA_HEADS * GRID_W, nwin)
        e = jnp.exp(s - jnp.max(s, axis=-1, keepdims=True))
        inv = 1.0 / jnp.sum(e, axis=-1, keepdims=True)
        eb = e.astype(bf16)
        for p in range(npair):
            cols = slice(p * LANES, (p + 1) * LANES)
            vp = v_ref[pl.ds(start, nwin), cols]
            r0, r1, r2 = 2 * p * GRID_W, (2 * p + 1) * GRID_W, (2 * p + 2) * GRID_W
            even = _dot(eb[r0:r1], vp) * inv[r0:r1]
            odd = _dot(eb[r1:r2], vp) * inv[r1:r2]
            o_ref[tok, cols] = jnp.where(lo, even, odd).astype(o_ref.dtype)


def neighbourhood_attention(proj, bias_table, *, B, S):
    T = B * S
    rows = S // GRID_W
    nwin = NA_KH * GRID_W
    R = NA_ROWS_PER_STEP
    steps = rows // R
    assert rows % R == 0 and rows >= NA_KH

    def group(b, g):
        first = (g == 0).astype(jnp.int32)
        last = (g == steps - 1).astype(jnp.int32)
        return (1 - first + last, 0, 0, 0, 0)

    return pl.pallas_call(
        functools.partial(_na_kernel, rows=rows),
        grid=(B, steps),
        in_specs=[pl.BlockSpec((R * GRID_W, NA_W), lambda b, g: (b * steps + g, 0)),
                  pl.BlockSpec((S, NA_W), lambda b, g: (b, 1)),
                  pl.BlockSpec((S, NA_W), lambda b, g: (b, 2)),
                  pl.BlockSpec((None, R, NA_HEADS, GRID_W, nwin), group)],
        out_specs=pl.BlockSpec((R * GRID_W, NA_W), lambda b, g: (b * steps + g, 0)),
        out_shape=jax.ShapeDtypeStruct((T, NA_W), bf16),
        compiler_params=pltpu.CompilerParams(
            dimension_semantics=("parallel", "arbitrary"), vmem_limit_bytes=VMEM_LIMIT),
        name="neighbourhood_attention",
    )(proj, proj, proj, bias_table)


def na_bias_table(rpb):
    qcol = np.arange(GRID_W)[:, None]
    kcol = np.arange(GRID_W)[None, :]
    cs = np.clip(qcol - NA_KW // 2, 0, GRID_W - NA_KW)
    in_win = (kcol >= cs) & (kcol < cs + NA_KW)
    dc = np.clip(kcol - qcol + NA_KW - 1, 0, 2 * NA_KW - 2)
    half = NA_KH // 2
    assert NA_ROWS_PER_STEP == half
    offs = np.stack([np.arange(half), np.full(half, half), half + np.arange(half)])
    dr = np.arange(NA_KH)[None, None, :] - offs[:, :, None] + NA_KH - 1
    tab = rpb.astype(f32)[:, dr][..., dc]
    tab = jnp.where(jnp.asarray(in_win), tab, NEG_INF)
    tab = tab.transpose(1, 2, 0, 4, 3, 5)
    return tab.reshape(3, NA_ROWS_PER_STEP, NA_HEADS, GRID_W, NA_KH * GRID_W)


def _diff_position_features(pos, lane_in_half, first):
    f = lane_in_half - first
    lo = jnp.bitwise_and(pos, DIFF_POS_RADIX - 1)
    hi = pos - lo
    val = jnp.where(jnp.bitwise_and(f, 1) == 0, hi, lo)
    return jnp.where((f >= 0) & (f < 2 * DIFF_SLOPE_PARTS), val, 0).astype(f32)


def _diff_kernel(lam_ref, q_ref, qnext_ref, k_ref, v_ref, cvec_ref, nw_ref, o_ref,
                 kaug_ref, qv_ref, s_even_ref, s_odd_ref, m_even_ref, m_odd_ref, *, S, out_scale):
    tq = DIFF_TQ
    nkt = S // tq
    qi = pl.program_id(2)
    lane = lax.broadcasted_iota(jnp.int32, (1, LANES), 1)
    lo = lane < HEAD_DIM
    lane_in_half = jnp.bitwise_and(lane, HEAD_DIM - 1)
    cq = cvec_ref[0:1, :]
    ck = cvec_ref[1:2, :]

    def key_rows(tile, r):
        t = jnp.bitwise_and(tile + r, nkt - 1)
        return pl.ds(pl.multiple_of(t * tq, tq), tq)

    def halves_max(x):
        return jnp.maximum(x[:, :LANES], x[:, LANES:])

    def pipeline(q_blk_ref, nxt_tile, nxt_s_ref, nxt_m_ref, cur_tile, cur_s_ref, cur_m_ref):
        qs = (q_blk_ref[...].astype(f32) * (HEAD_DIM ** -0.5 * LOG2E)).astype(bf16)
        qpos = nxt_tile * tq + lax.broadcasted_iota(jnp.int32, (tq, 1), 0)
        fq = _diff_position_features(qpos, lane_in_half, 0) + cq
        zero = jnp.zeros_like(qs)
        for v, f in enumerate((fq.astype(bf16), zero, (-fq).astype(bf16))):
            qv_ref[v] = jnp.where(lo, qs, f)
            qv_ref[3 + v] = jnp.where(lo, f, qs)
        d = (lax.broadcasted_iota(jnp.int32, (tq, tq), 0)
             - lax.broadcasted_iota(jnp.int32, (tq, tq), 1)).astype(f32)
        diag_bias = -cvec_ref[2:3, 0:1] * jnp.abs(d)

        if cur_s_ref is not None:
            m_cur = [jnp.max(cur_m_ref[c], axis=1, keepdims=True) for c in range(2)]
            l_part = [jnp.zeros((tq, LANES), f32) for _ in range(2)]
        m_acc = [None, None]
        for r in range(nkt):
            if cur_s_ref is not None:
                for c in range(2):
                    e = jnp.exp2(cur_s_ref[c, r] - m_cur[c])
                    cur_s_ref[c, r] = e
                    l_part[c] = l_part[c] + (e[:, :LANES] + e[:, LANES:])
            rows = key_rows(nxt_tile, r)
            variant = 1 if r == 0 else jnp.where(nxt_tile + r >= nkt, 0, 2)
            for c in range(2):
                s = _nt_dot(qv_ref[3 * c + variant], kaug_ref[c, rows, :])
                if r == 0:
                    s = s + diag_bias
                nxt_s_ref[c, r] = s
                m_acc[c] = halves_max(s) if r == 0 else jnp.maximum(m_acc[c], halves_max(s))
        for c in range(2):
            nxt_m_ref[c] = m_acc[c]
        if cur_s_ref is None:
            return

        r1 = 1.0 / jnp.sum(l_part[0], axis=1, keepdims=True)
        r2 = lam_ref[0] / jnp.sum(l_part[1], axis=1, keepdims=True)
        out = jnp.zeros((tq, LANES), f32)
        for r in range(nkt):
            a = (cur_s_ref[0, r] * r1 - cur_s_ref[1, r] * r2).astype(bf16)
            out += _dot(a, v_ref[key_rows(cur_tile, r), :])
        y = out * lax.rsqrt(jnp.mean(out * out, axis=-1, keepdims=True) + NORM_EPS)
        o_ref[...] = (y * nw_ref[...] * out_scale).astype(o_ref.dtype)

    @pl.when(qi == 0)
    def _():
        kpos = lax.broadcasted_iota(jnp.int32, (S, 1), 0)
        fk = (_diff_position_features(kpos, lane_in_half, 2 * DIFF_SLOPE_PARTS) + ck).astype(bf16)
        k = k_ref[...]
        kaug_ref[0] = jnp.where(lo, k, fk)
        kaug_ref[1] = jnp.where(lo, fk, k)
        pipeline(q_ref, qi, s_even_ref, m_even_ref, None, None, None)

    @pl.when(qi % 2 == 0)
    def _():
        pipeline(qnext_ref, qi + 1, s_odd_ref, m_odd_ref, qi, s_even_ref, m_even_ref)

    @pl.when(qi % 2 == 1)
    def _():
        pipeline(qnext_ref, qi + 1, s_even_ref, m_even_ref, qi, s_odd_ref, m_odd_ref)


def diff_slope_vectors(slopes):
    c = slopes.astype(f32) * LOG2E
    parts = []
    rest = c
    for _ in range(DIFF_SLOPE_PARTS):
        p = rest.astype(bf16).astype(f32)
        parts.append(p)
        rest = rest - p
    parts = jnp.stack(parts, axis=1)
    lane_in_half = np.arange(LANES) % HEAD_DIM
    n = 2 * DIFF_SLOPE_PARTS
    q_sel = np.where((lane_in_half >= n) & (lane_in_half < 2 * n), (lane_in_half - n) // 2, -1)
    k_sel = np.where(lane_in_half < n, lane_in_half // 2, -1)
    zero = jnp.zeros((slopes.shape[0], 1), f32)
    padded = jnp.concatenate([parts, zero], axis=1)
    cq = padded[:, q_sel]
    ck = -padded[:, k_sel]
    return jnp.stack([cq, ck, jnp.broadcast_to(c[:, None], cq.shape)], axis=1)


def diff_attention(proj, cvec, lam, norm_w, *, B, S, lam_init):
    T = B * S
    tq = DIFF_TQ
    nq = S // tq
    assert S <= DIFF_POS_RADIX * DIFF_POS_RADIX and nq & (nq - 1) == 0
    blk0 = DIFF_COL0 // LANES
    smem = pl.BlockSpec(memory_space=pltpu.SMEM)
    return pl.pallas_call(
        functools.partial(_diff_kernel, S=S, out_scale=1.0 - lam_init),
        grid=(B, DIFF_HEADS, nq),
        in_specs=[smem,
                  pl.BlockSpec((tq, LANES), lambda b, h, i: (b * nq + i, blk0 + h)),
                  pl.BlockSpec((tq, LANES), lambda b, h, i: (b * nq + jnp.minimum(i + 1, nq - 1), blk0 + h)),
                  pl.BlockSpec((S, LANES), lambda b, h, i: (b, blk0 + DIFF_HEADS + h)),
                  pl.BlockSpec((S, LANES), lambda b, h, i: (b, blk0 + 2 * DIFF_HEADS + h)),
                  pl.BlockSpec((None, 3, LANES), lambda b, h, i: (h, 0, 0)),
                  pl.BlockSpec((1, LANES), lambda b, h, i: (0, h))],
        out_specs=pl.BlockSpec((tq, LANES), lambda b, h, i: (b * nq + i, h)),
        out_shape=jax.ShapeDtypeStruct((T, DIFF_W), bf16),
        scratch_shapes=[pltpu.VMEM((2, S, LANES), bf16),
                        pltpu.VMEM((6, tq, LANES), bf16),
                        pltpu.VMEM((2, nq, tq, tq), f32),
                        pltpu.VMEM((2, nq, tq, tq), f32),
                        pltpu.VMEM((2, tq, LANES), f32),
                        pltpu.VMEM((2, tq, LANES), f32)],
        compiler_params=pltpu.CompilerParams(
            dimension_semantics=("parallel", "parallel", "arbitrary"), vmem_limit_bytes=VMEM_LIMIT),
        name="diff_attention",
    )(lam, proj, proj, proj, proj, cvec, norm_w.reshape(1, DIFF_W))


def _out_proj_kernel(x_ref, r_ref, n_ref, d_ref, w_ref, o_ref):
    acc = _dot(r_ref[...], w_ref[0:RET_W, :])
    acc += _dot(n_ref[...], w_ref[RET_W:RET_W + NA_W, :])
    acc += _dot(d_ref[...], w_ref[RET_W + NA_W:, :])
    o_ref[...] = x_ref[...] + acc


def out_proj(x, ret, na, diff, w_out, *, tm=512):
    T, D = x.shape
    row = lambda i: (i, 0)
    return pl.pallas_call(
        _out_proj_kernel,
        grid=(T // tm,),
        in_specs=[pl.BlockSpec((tm, D), row),
                  pl.BlockSpec((tm, RET_W), row),
                  pl.BlockSpec((tm, NA_W), row),
                  pl.BlockSpec((tm, DIFF_W), row),
                  pl.BlockSpec((D, D), lambda i: (0, 0))],
        out_specs=pl.BlockSpec((tm, D), row),
        out_shape=jax.ShapeDtypeStruct((T, D), f32),
        compiler_params=pltpu.CompilerParams(
            dimension_semantics=("parallel",), vmem_limit_bytes=VMEM_LIMIT),
        name="out_proj",
    )(x, ret, na, diff, w_out)


def _ffn_down_kernel(x_ref, a_ref, g_ref, gp_ref, gn_ref, cw_ref, cb_ref, wd_ref, fw_ref, o_ref,
                     *, S, tm, final_norm):
    i = pl.program_id(0)
    k = pl.program_id(1)

    @pl.when(k == 0)
    def _():
        o_ref[...] = x_ref[...]

    g = g_ref[...].astype(f32)
    row = lax.broadcasted_iota(jnp.int32, (tm, 1), 0)
    t0 = i * tm
    at_seq_start = (t0 % S) == 0
    at_seq_end = ((t0 + tm) % S) == 0
    prev = jnp.where(at_seq_start, 0.0, gp_ref[BF16_SUBLANES - 1:BF16_SUBLANES, :].astype(f32))
    nxt = jnp.where(at_seq_end, 0.0, gn_ref[0:1, :].astype(f32))
    g_up = jnp.where(row == 0, prev, pltpu.roll(g, 1, 0))
    g_dn = jnp.where(row == tm - 1, nxt, pltpu.roll(g, tm - 1, 0))
    c = g_up * cw_ref[0:1, :] + g * cw_ref[1:2, :] + g_dn * cw_ref[2:3, :] + cb_ref[...]
    gated = (jax.nn.gelu(c) * a_ref[...].astype(f32)).astype(bf16)
    o_ref[...] += _dot(gated, wd_ref[...])

    if final_norm:
        @pl.when(k == pl.num_programs(1) - 1)
        def _():
            x = o_ref[...]
            y = x * lax.rsqrt(jnp.mean(x * x, axis=-1, keepdims=True) + NORM_EPS)
            o_ref[...] = y * fw_ref[...]


def ffn_down(x, u, conv_w, conv_b, w_down, final_w, *, S, final_norm, tm=512, tk=1408):
    T, D = x.shape
    nk = D_FF // tk
    rb = tm // BF16_SUBLANES
    last_rb = T // BF16_SUBLANES - 1
    return pl.pallas_call(
        functools.partial(_ffn_down_kernel, S=S, tm=tm, final_norm=final_norm),
        grid=(T // tm, nk),
        in_specs=[pl.BlockSpec((tm, D), lambda i, k: (i, 0)),
                  pl.BlockSpec((tm, tk), lambda i, k: (i, k)),
                  pl.BlockSpec((tm, tk), lambda i, k: (i, nk + k)),
                  pl.BlockSpec((BF16_SUBLANES, tk), lambda i, k: (jnp.maximum(i * rb - 1, 0), nk + k)),
                  pl.BlockSpec((BF16_SUBLANES, tk), lambda i, k: (jnp.minimum((i + 1) * rb, last_rb), nk + k)),
                  pl.BlockSpec((3, tk), lambda i, k: (0, k)),
                  pl.BlockSpec((1, tk), lambda i, k: (0, k)),
                  pl.BlockSpec((tk, D), lambda i, k: (k, 0)),
                  pl.BlockSpec((1, D), lambda i, k: (0, 0))],
        out_specs=pl.BlockSpec((tm, D), lambda i, k: (i, 0)),
        out_shape=jax.ShapeDtypeStruct((T, D), f32),
        compiler_params=pltpu.CompilerParams(
            dimension_semantics=("parallel", "arbitrary"), vmem_limit_bytes=VMEM_LIMIT),
        name="ffn_down",
    )(x, u, u, u, u, conv_w, conv_b.reshape(1, D_FF), w_down, final_w.reshape(1, D))


def _prepare_layer(l, norm1_w, w_in, ret_decay_fwd, ret_decay_bwd, ret_norm_w, na_rpb,
                   lq1, lk1, lq2, lk2, diff_norm_w, w_out, norm2_w, ffn_w_up, ffn_conv_w,
                   ffn_conv_b, ffn_w_down):
    r0, n0, d0 = 0, 4 * RET_W, 4 * RET_W + 3 * NA_W
    w = w_in[l]
    w_perm = jnp.concatenate([w[:, n0:d0], w[:, d0:], w[:, r0:n0]], axis=1).astype(bf16)
    lam_init = 0.8 - 0.6 * math.exp(-0.3 * l)
    lam = (jnp.exp(jnp.sum(lq1[l].astype(f32) * lk1[l].astype(f32)))
           - jnp.exp(jnp.sum(lq2[l].astype(f32) * lk2[l].astype(f32))) + lam_init)
    return dict(
        norm1_w=norm1_w[l].astype(f32), w_in=w_perm,
        lgf=jax.nn.log_sigmoid(ret_decay_fwd[l].astype(f32)),
        lgb=jax.nn.log_sigmoid(ret_decay_bwd[l].astype(f32)),
        ret_norm_w=ret_norm_w[l].astype(f32),
        na_bias=na_bias_table(na_rpb[l]),
        lam=lam.reshape(1), lam_init=lam_init, diff_norm_w=diff_norm_w[l].astype(f32),
        w_out=w_out[l].astype(bf16), norm2_w=norm2_w[l].astype(f32),
        w_up=ffn_w_up[l].astype(bf16), conv_w=ffn_conv_w[l].astype(f32),
        conv_b=ffn_conv_b[l].astype(f32), w_down=ffn_w_down[l].astype(bf16))


def _trunk(x, layers, final_norm_w, slope_vecs):
    B, S, D = x.shape
    x = x.reshape(B * S, D)
    for l, p in enumerate(layers):
        proj = norm_matmul(x, p["norm1_w"], p["w_in"])
        ret = retention(proj, p["lgf"], p["lgb"], p["ret_norm_w"], B=B, S=S)
        na = neighbourhood_attention(proj, p["na_bias"], B=B, S=S)
        diff = diff_attention(proj, slope_vecs, p["lam"], p["diff_norm_w"], B=B, S=S, lam_init=p["lam_init"])
        x = out_proj(x, ret, na, diff, p["w_out"])
        u = norm_matmul(x, p["norm2_w"], p["w_up"])
        x = ffn_down(x, u, p["conv_w"], p["conv_b"], p["w_down"], final_norm_w.astype(f32),
                     S=S, final_norm=(l == len(layers) - 1))
    return x.reshape(B, S, D)


def kernel(x_prompt, x_sample, norm1_w, w_in, ret_decay_fwd, ret_decay_bwd, ret_norm_w, na_rpb, diff_lambda_q1, diff_lambda_k1, diff_lambda_q2, diff_lambda_k2, diff_norm_w, w_out, norm2_w, ffn_w_up, ffn_conv_w, ffn_conv_b, ffn_w_down, final_norm_w):
    layers = [_prepare_layer(l, norm1_w, w_in, ret_decay_fwd, ret_decay_bwd, ret_norm_w, na_rpb,
                             diff_lambda_q1, diff_lambda_k1, diff_lambda_q2, diff_lambda_k2,
                             diff_norm_w, w_out, norm2_w, ffn_w_up, ffn_conv_w, ffn_conv_b, ffn_w_down)
              for l in range(w_in.shape[0])]
    slopes = 2.0 ** (-8.0 * (jnp.arange(DIFF_HEADS, dtype=f32) + 1.0) / DIFF_HEADS)
    slope_vecs = diff_slope_vectors(slopes)
    y_prompt = _trunk(x_prompt, layers, final_norm_w, slope_vecs)
    y_sample = _trunk(x_sample, layers, final_norm_w, slope_vecs)
    return (y_prompt, y_sample)
```

```python
import functools
import math

import jax
import jax.numpy as jnp
import numpy as np
from jax import lax
from jax.experimental import pallas as pl
from jax.experimental.pallas import tpu as pltpu

D_MODEL = 2048
DEPTH = 2
HEAD_DIM = 64
RET_HEADS = 8
RET_W = RET_HEADS * HEAD_DIM
NA_HEADS = 12
NA_W = NA_HEADS * HEAD_DIM
GRID_W = 64
NA_KH = 8
NA_KW = 16
DIFF_HEADS = 6
DIFF_V_DIM = 2 * HEAD_DIM
DIFF_W = DIFF_HEADS * DIFF_V_DIM
D_IN = 4 * RET_W + 3 * NA_W + 3 * DIFF_W
D_FF = 5632
NORM_EPS = 1e-6
NEG_INF = -1e30

LANES = 128
F32_SUBLANES = 8
MXU_WIDTH = 256
RET_CHUNK = 256
RET_UNROLL = 4
NA_ROWS_PER_STEP = 4
DIFF_TQ = 256
DIFF_POS_RADIX = 64
DIFF_SLOPE_PARTS = 3
LOG2E = 1.4426950408889634
VMEM_LIMIT = 56 * 1024 * 1024

NA_COL0 = 0
DIFF_COL0 = 3 * NA_W
RET_COL0 = 3 * NA_W + 3 * DIFF_W

f32 = jnp.float32
bf16 = jnp.bfloat16


def _nt_dot(a, b):
    return lax.dot_general(a, b, (((1,), (1,)), ((), ())), preferred_element_type=f32)


def _tn_dot(a, b):
    return lax.dot_general(a, b, (((0,), (0,)), ((), ())), preferred_element_type=f32)


def _dot(a, b):
    return jnp.dot(a, b, preferred_element_type=f32)


def _norm_matmul_kernel(x_ref, nw_ref, w_ref, o_ref, h_ref):
    @pl.when(pl.program_id(1) == 0)
    def _():
        x = x_ref[...]
        y = x * lax.rsqrt(jnp.mean(x * x, axis=-1, keepdims=True) + NORM_EPS)
        h_ref[...] = (y * nw_ref[...]).astype(bf16)

    o_ref[...] = _dot(h_ref[...], w_ref[...]).astype(o_ref.dtype)


def norm_matmul(x, nw, w, *, tm=1024, tn=512):
    T, D = x.shape
    N = w.shape[1]
    return pl.pallas_call(
        _norm_matmul_kernel,
        grid=(T // tm, N // tn),
        in_specs=[
            pl.BlockSpec((tm, D), lambda i, j: (i, 0)),
            pl.BlockSpec((1, D), lambda i, j: (0, 0)),
            pl.BlockSpec((D, tn), lambda i, j: (0, j)),
        ],
        out_specs=pl.BlockSpec((tm, tn), lambda i, j: (i, j)),
        out_shape=jax.ShapeDtypeStruct((T, N), bf16),
        scratch_shapes=[pltpu.VMEM((tm, D), bf16)],
        compiler_params=pltpu.CompilerParams(
            dimension_semantics=("parallel", "arbitrary"), vmem_limit_bytes=VMEM_LIMIT),
        name="norm_matmul",
    )(x, nw.reshape(1, D), w)


def _retention_kernel(lgf_ref, lgb_ref, q_ref, k_ref, v_ref, g_ref, nw_ref, o_ref, acc_ref, *, S):
    C = RET_CHUNK
    n = S // C
    pair = pl.program_id(1)
    scale = HEAD_DIM ** -0.5

    lane = lax.broadcasted_iota(jnp.int32, (1, LANES), 1)
    lo = lane < HEAD_DIM
    lgf = jnp.where(lo, lgf_ref[2 * pair], lgf_ref[2 * pair + 1])
    lgb = jnp.where(lo, lgb_ref[2 * pair], lgb_ref[2 * pair + 1])

    pos = lax.broadcasted_iota(jnp.int32, (C, 1), 0).astype(f32)
    qdec_f = jnp.exp(lgf * (pos + 1.0))
    kdec_f = jnp.exp(lgf * (C - 1.0 - pos)) * scale
    cdec_f = jnp.exp(lgf * float(C))
    qdec_b = jnp.exp(lgb * (C - pos))
    kdec_b = jnp.exp(lgb * pos) * scale
    cdec_b = jnp.exp(lgb * float(C))

    ri = lax.broadcasted_iota(jnp.int32, (C, C), 0)
    ci = lax.broadcasted_iota(jnp.int32, (C, C), 1)
    delta = (ri - ci).astype(f32)

    def decay_mat(h):
        fwd = jnp.exp(lgf_ref[2 * pair + h] * jnp.maximum(delta, 0.0))
        bwd = jnp.exp(lgb_ref[2 * pair + h] * jnp.maximum(-delta, 0.0))
        return jnp.where(delta >= 0, fwd, bwd) * scale

    dmat0 = decay_mat(0)
    dmat1 = decay_mat(1)

    r2 = lax.broadcasted_iota(jnp.int32, (LANES, LANES), 0) // HEAD_DIM
    c2 = lax.broadcasted_iota(jnp.int32, (LANES, LANES), 1) // HEAD_DIM
    same_head = r2 == c2
    avg = jnp.where(same_head, 1.0 / HEAD_DIM, 0.0).astype(bf16)

    def chunk(ref, c):
        return ref[pl.ds(pl.multiple_of(c * C, C), C), :]

    def fwd_body(c, state):
        qc, kc, vc = chunk(q_ref, c), chunk(k_ref, c), chunk(v_ref, c)
        zero = jnp.zeros_like(qc)
        in0 = _nt_dot(jnp.where(lo, qc, zero), kc)
        in1 = _nt_dot(jnp.where(lo, zero, qc), kc)
        w = jnp.concatenate([(in0 * dmat0).astype(bf16), (in1 * dmat1).astype(bf16)], axis=1)
        vv = jnp.concatenate([jnp.where(lo, vc, zero), jnp.where(lo, zero, vc)], axis=0)
        intra = _dot(w, vv)
        qd = (qc.astype(f32) * qdec_f).astype(bf16)
        cross = _dot(qd, state.astype(bf16))
        acc_ref[pl.ds(pl.multiple_of(c * C, C), C), :] = intra + cross
        kd = (kc.astype(f32) * kdec_f).astype(bf16)
        upd = _tn_dot(kd, vc)
        return state * cdec_f + jnp.where(same_head, upd, 0.0)

    lax.fori_loop(0, n, fwd_body, jnp.zeros((LANES, LANES), f32), unroll=RET_UNROLL)

    def lane_mean(y):
        hi = y.astype(bf16)
        lo_part = (y - hi.astype(f32)).astype(bf16)
        return _dot(hi, avg) + _dot(lo_part, avg)

    def bwd_body(t, state):
        c = n - 1 - t
        qc, kc, vc = chunk(q_ref, c), chunk(k_ref, c), chunk(v_ref, c)
        qd = (qc.astype(f32) * qdec_b).astype(bf16)
        y = chunk(acc_ref, c) + _dot(qd, state.astype(bf16))
        y = y - lane_mean(y)
        y = y * lax.rsqrt(lane_mean(y * y) + NORM_EPS)
        gc = chunk(g_ref, c).astype(f32)
        out = jax.nn.silu(gc) * (y * nw_ref[...])
        o_ref[pl.ds(pl.multiple_of(c * C, C), C), :] = out.astype(o_ref.dtype)
        kd = (kc.astype(f32) * kdec_b).astype(bf16)
        upd = _tn_dot(kd, vc)
        return state * cdec_b + jnp.where(same_head, upd, 0.0)

    lax.fori_loop(0, n, bwd_body, jnp.zeros((LANES, LANES), f32), unroll=RET_UNROLL)


def retention(proj, lgf, lgb, norm_w, *, B, S):
    T = B * S
    blk0 = RET_COL0 // LANES
    npair = RET_HEADS // 2

    def col(seg):
        return lambda b, p: (b, blk0 + seg * npair + p)

    smem = pl.BlockSpec(memory_space=pltpu.SMEM)
    return pl.pallas_call(
        functools.partial(_retention_kernel, S=S),
        grid=(B, npair),
        in_specs=[smem, smem,
                  pl.BlockSpec((S, LANES), col(0)),
                  pl.BlockSpec((S, LANES), col(1)),
                  pl.BlockSpec((S, LANES), col(2)),
                  pl.BlockSpec((S, LANES), col(3)),
                  pl.BlockSpec((1, LANES), lambda b, p: (0, p))],
        out_specs=pl.BlockSpec((S, LANES), lambda b, p: (b, p)),
        out_shape=jax.ShapeDtypeStruct((T, RET_W), bf16),
        scratch_shapes=[pltpu.VMEM((S, LANES), f32)],
        compiler_params=pltpu.CompilerParams(
            dimension_semantics=("parallel", "parallel"), vmem_limit_bytes=VMEM_LIMIT),
        name="retention",
    )(lgf, lgb, proj, proj, proj, proj, norm_w.reshape(1, RET_W))


def _na_window_start(r, rows):
    return jnp.clip(r - NA_KH // 2, 0, rows - NA_KH)


def _na_kernel(q_ref, k_ref, v_ref, bias_ref, o_ref, *, rows):
    g = pl.program_id(1)
    nwin = NA_KH * GRID_W
    lane = lax.broadcasted_iota(jnp.int32, (1, LANES), 1)
    lo = lane < HEAD_DIM
    scale = HEAD_DIM ** -0.5
    npair = NA_HEADS // 2
    for j in range(NA_ROWS_PER_STEP):
        r = g * NA_ROWS_PER_STEP + j
        start = pl.multiple_of(_na_window_start(r, rows) * GRID_W, GRID_W)
        tok = slice(j * GRID_W, (j + 1) * GRID_W)
        parts = []
        for p in range(npair):
            cols = slice(p * LANES, (p + 1) * LANES)
            qp = (q_ref[tok, cols].astype(f32) * scale).astype(bf16)
            kp = k_ref[pl.ds(start, nwin), cols]
            zero = jnp.zeros_like(qp)
            parts.append(_nt_dot(jnp.where(lo, qp, zero), kp))
            parts.append(_nt_dot(jnp.where(lo, zero, qp), kp))
        bias = jnp.concatenate([bias_ref[h, j] for h in range(NA_HEADS)], axis=0)
        s = jnp.concatenate(parts, axis=0) + bias
        e = jnp.exp(s - jnp.max(s, axis=-1, keepdims=True))
        inv = 1.0 / jnp.sum(e, axis=-1, keepdims=True)
        eb = e.astype(bf16)
        for p in range(npair):
            cols = slice(p * LANES, (p + 1) * LANES)
            vp = v_ref[pl.ds(start, nwin), cols]
            r0, r1, r2 = 2 * p * GRID_W, (2 * p + 1) * GRID_W, (2 * p + 2) * GRID_W
            even = _dot(eb[r0:r1], vp) * inv[r0:r1]
            odd = _dot(eb[r1:r2], vp) * inv[r1:r2]
            o_ref[tok, cols] = jnp.where(lo, even, odd).astype(o_ref.dtype)


def neighbourhood_attention(proj, bias_table, *, B, S):
    T = B * S
    rows = S // GRID_W
    nwin = NA_KH * GRID_W
    R = NA_ROWS_PER_STEP
    steps = rows // R
    assert rows % R == 0 and rows >= NA_KH

    def group(b, g):
        first = (g == 0).astype(jnp.int32)
        last = (g == steps - 1).astype(jnp.int32)
        return (0, 1 - first + last, 0, 0, 0)

    return pl.pallas_call(
        functools.partial(_na_kernel, rows=rows),
        grid=(B, steps),
        in_specs=[pl.BlockSpec((R * GRID_W, NA_W), lambda b, g: (b * steps + g, 0)),
                  pl.BlockSpec((S, NA_W), lambda b, g: (b, 1)),
                  pl.BlockSpec((S, NA_W), lambda b, g: (b, 2)),
                  pl.BlockSpec((NA_HEADS, None, R, GRID_W, nwin), group)],
        out_specs=pl.BlockSpec((R * GRID_W, NA_W), lambda b, g: (b * steps + g, 0)),
        out_shape=jax.ShapeDtypeStruct((T, NA_W), bf16),
        compiler_params=pltpu.CompilerParams(
            dimension_semantics=("parallel", "arbitrary"), vmem_limit_bytes=VMEM_LIMIT),
        name="neighbourhood_attention",
    )(proj, proj, proj, bias_table)


def na_bias_table(rpb):
    qcol = np.arange(GRID_W)[:, None]
    kcol = np.arange(GRID_W)[None, :]
    cs = np.clip(qcol - NA_KW // 2, 0, GRID_W - NA_KW)
    in_win = (kcol >= cs) & (kcol < cs + NA_KW)
    dc = np.clip(kcol - qcol + NA_KW - 1, 0, 2 * NA_KW - 2)
    half = NA_KH // 2
    assert NA_ROWS_PER_STEP == half
    offs = np.stack([np.arange(half), np.full(half, half), half + np.arange(half)])
    dr = np.arange(NA_KH)[None, None, :] - offs[:, :, None] + NA_KH - 1
    ncol = 2 * NA_KW - 1
    flat = dr[:, :, None, :, None] * ncol + dc[None, None, :, None, :]
    flat = flat.reshape(3, NA_ROWS_PER_STEP, GRID_W, NA_KH * GRID_W)
    mask = np.broadcast_to(in_win[:, None, :], (GRID_W, NA_KH, GRID_W)).reshape(GRID_W, NA_KH * GRID_W)
    tab = rpb.astype(f32).reshape(NA_HEADS, -1)[:, flat]
    return jnp.where(jnp.asarray(mask), tab, NEG_INF)


def _diff_position_features(pos, lane_in_half, first):
    f = lane_in_half - first
    lo = jnp.bitwise_and(pos, DIFF_POS_RADIX - 1)
    hi = pos - lo
    val = jnp.where(jnp.bitwise_and(f, 1) == 0, hi, lo)
    return jnp.where((f >= 0) & (f < 2 * DIFF_SLOPE_PARTS), val, 0).astype(f32)


def _diff_kernel(lam_ref, q_ref, qnext_ref, k_ref, v_ref, cvec_ref, nw_ref, o_ref,
                 kaug_ref, qv_ref, s_even_ref, s_odd_ref, m_even_ref, m_odd_ref, *, S, out_scale):
    tq = DIFF_TQ
    nkt = S // tq
    qi = pl.program_id(2)
    lane = lax.broadcasted_iota(jnp.int32, (1, LANES), 1)
    lo = lane < HEAD_DIM
    lane_in_half = jnp.bitwise_and(lane, HEAD_DIM - 1)
    cq = cvec_ref[0:1, :]
    ck = cvec_ref[1:2, :]

    def key_rows(tile, r):
        t = jnp.bitwise_and(tile + r, nkt - 1)
        return pl.ds(pl.multiple_of(t * tq, tq), tq)

    def halves_max(x):
        return jnp.maximum(x[:, :LANES], x[:, LANES:])

    def pipeline(q_blk_ref, nxt_tile, nxt_s_ref, nxt_m_ref, cur_tile, cur_s_ref, cur_m_ref):
        qs = (q_blk_ref[...].astype(f32) * (HEAD_DIM ** -0.5 * LOG2E)).astype(bf16)
        qpos = nxt_tile * tq + lax.broadcasted_iota(jnp.int32, (tq, 1), 0)
        fq = _diff_position_features(qpos, lane_in_half, 0) + cq
        zero = jnp.zeros_like(qs)
        for v, f in enumerate((fq.astype(bf16), zero, (-fq).astype(bf16))):
            qv_ref[v] = jnp.where(lo, qs, f)
            qv_ref[3 + v] = jnp.where(lo, f, qs)
        d = (lax.broadcasted_iota(jnp.int32, (tq, tq), 0)
             - lax.broadcasted_iota(jnp.int32, (tq, tq), 1)).astype(f32)
        diag_bias = -cvec_ref[2:3, 0:1] * jnp.abs(d)

        if cur_s_ref is not None:
            m_cur = [jnp.max(cur_m_ref[c], axis=1, keepdims=True) for c in range(2)]
            l_part = [jnp.zeros((tq, LANES), f32) for _ in range(2)]
        m_acc = [None, None]
        for r in range(nkt):
            if cur_s_ref is not None:
                for c in range(2):
                    e = jnp.exp2(cur_s_ref[c, r] - m_cur[c])
                    cur_s_ref[c, r] = e
                    l_part[c] = l_part[c] + (e[:, :LANES] + e[:, LANES:])
            rows = key_rows(nxt_tile, r)
            variant = 1 if r == 0 else jnp.where(nxt_tile + r >= nkt, 0, 2)
            for c in range(2):
                s = _nt_dot(qv_ref[3 * c + variant], kaug_ref[c, rows, :])
                if r == 0:
                    s = s + diag_bias
                nxt_s_ref[c, r] = s
                m_acc[c] = halves_max(s) if r == 0 else jnp.maximum(m_acc[c], halves_max(s))
        for c in range(2):
            nxt_m_ref[c] = m_acc[c]
        if cur_s_ref is None:
            return

        r1 = 1.0 / jnp.sum(l_part[0], axis=1, keepdims=True)
        r2 = lam_ref[0] / jnp.sum(l_part[1], axis=1, keepdims=True)
        out = None
        for r in range(nkt):
            a = (cur_s_ref[0, r] * r1 - cur_s_ref[1, r] * r2).astype(bf16)
            pv = _dot(a, v_ref[key_rows(cur_tile, r), :])
            out = pv if out is None else out + pv
        y = out * lax.rsqrt(jnp.mean(out * out, axis=-1, keepdims=True) + NORM_EPS)
        o_ref[...] = (y * nw_ref[...] * out_scale).astype(o_ref.dtype)

    @pl.when(qi == 0)
    def _():
        kpos = lax.broadcasted_iota(jnp.int32, (S, 1), 0)
        fk = (_diff_position_features(kpos, lane_in_half, 2 * DIFF_SLOPE_PARTS) + ck).astype(bf16)
        k = k_ref[...]
        kaug_ref[0] = jnp.where(lo, k, fk)
        kaug_ref[1] = jnp.where(lo, fk, k)
        pipeline(q_ref, qi, s_even_ref, m_even_ref, None, None, None)

    @pl.when(qi % 2 == 0)
    def _():
        pipeline(qnext_ref, qi + 1, s_odd_ref, m_odd_ref, qi, s_even_ref, m_even_ref)

    @pl.when(qi % 2 == 1)
    def _():
        pipeline(qnext_ref, qi + 1, s_even_ref, m_even_ref, qi, s_odd_ref, m_odd_ref)


def diff_slope_vectors(slopes):
    c = slopes.astype(f32) * LOG2E
    parts = []
    rest = c
    for _ in range(DIFF_SLOPE_PARTS):
        p = rest.astype(bf16).astype(f32)
        parts.append(p)
        rest = rest - p
    parts = jnp.stack(parts, axis=1)
    lane_in_half = np.arange(LANES) % HEAD_DIM
    n = 2 * DIFF_SLOPE_PARTS
    q_sel = np.where((lane_in_half >= n) & (lane_in_half < 2 * n), (lane_in_half - n) // 2, -1)
    k_sel = np.where(lane_in_half < n, lane_in_half // 2, -1)
    zero = jnp.zeros((slopes.shape[0], 1), f32)
    padded = jnp.concatenate([parts, zero], axis=1)
    cq = padded[:, q_sel]
    ck = -padded[:, k_sel]
    return jnp.stack([cq, ck, jnp.broadcast_to(c[:, None], cq.shape)], axis=1)


def diff_attention(proj, cvec, lam, norm_w, *, B, S, lam_init):
    T = B * S
    tq = DIFF_TQ
    nq = S // tq
    assert S <= DIFF_POS_RADIX * DIFF_POS_RADIX and nq & (nq - 1) == 0
    blk0 = DIFF_COL0 // LANES
    smem = pl.BlockSpec(memory_space=pltpu.SMEM)
    return pl.pallas_call(
        functools.partial(_diff_kernel, S=S, out_scale=1.0 - lam_init),
        grid=(B, DIFF_HEADS, nq),
        in_specs=[smem,
                  pl.BlockSpec((tq, LANES), lambda b, h, i: (b * nq + i, blk0 + h)),
                  pl.BlockSpec((tq, LANES), lambda b, h, i: (b * nq + jnp.minimum(i + 1, nq - 1), blk0 + h)),
                  pl.BlockSpec((S, LANES), lambda b, h, i: (b, blk0 + DIFF_HEADS + h)),
                  pl.BlockSpec((S, LANES), lambda b, h, i: (b, blk0 + 2 * DIFF_HEADS + h)),
                  pl.BlockSpec((None, 3, LANES), lambda b, h, i: (h, 0, 0)),
                  pl.BlockSpec((1, LANES), lambda b, h, i: (0, h))],
        out_specs=pl.BlockSpec((tq, LANES), lambda b, h, i: (b * nq + i, h)),
        out_shape=jax.ShapeDtypeStruct((T, DIFF_W), bf16),
        scratch_shapes=[pltpu.VMEM((2, S, LANES), bf16),
                        pltpu.VMEM((6, tq, LANES), bf16),
                        pltpu.VMEM((2, nq, tq, tq), f32),
                        pltpu.VMEM((2, nq, tq, tq), f32),
                        pltpu.VMEM((2, tq, LANES), f32),
                        pltpu.VMEM((2, tq, LANES), f32)],
        compiler_params=pltpu.CompilerParams(
            dimension_semantics=("parallel", "parallel", "arbitrary"), vmem_limit_bytes=VMEM_LIMIT),
        name="diff_attention",
    )(lam, proj, proj, proj, proj, cvec, norm_w.reshape(1, DIFF_W))


def _out_proj_kernel(x_ref, r_ref, n_ref, d_ref, w_ref, o_ref):
    acc = _dot(r_ref[...], w_ref[0:RET_W, :])
    acc += _dot(n_ref[...], w_ref[RET_W:RET_W + NA_W, :])
    acc += _dot(d_ref[...], w_ref[RET_W + NA_W:, :])
    o_ref[...] = x_ref[...] + acc


def out_proj(x, ret, na, diff, w_out, *, tm=512):
    T, D = x.shape
    row = lambda i: (i, 0)
    return pl.pallas_call(
        _out_proj_kernel,
        grid=(T // tm,),
        in_specs=[pl.BlockSpec((tm, D), row),
                  pl.BlockSpec((tm, RET_W), row),
                  pl.BlockSpec((tm, NA_W), row),
                  pl.BlockSpec((tm, DIFF_W), row),
                  pl.BlockSpec((D, D), lambda i: (0, 0))],
        out_specs=pl.BlockSpec((tm, D), row),
        out_shape=jax.ShapeDtypeStruct((T, D), f32),
        compiler_params=pltpu.CompilerParams(
            dimension_semantics=("parallel",), vmem_limit_bytes=VMEM_LIMIT),
        name="out_proj",
    )(x, ret, na, diff, w_out)


def _rms(x):
    return x * lax.rsqrt(jnp.mean(x * x, axis=-1, keepdims=True) + NORM_EPS)


def _ffn_gate_kernel(x_ref, xp_ref, xn_ref, nw_ref, wa_ref, wg_ref, cw_ref, cb_ref, o_ref, h_ref, hh_ref,
                     *, S, tm):
    i = pl.program_id(0)

    @pl.when(pl.program_id(1) == 0)
    def _():
        nw = nw_ref[...]
        h_ref[...] = (_rms(x_ref[...]) * nw).astype(bf16)
        hh_ref[0:F32_SUBLANES, :] = (_rms(xp_ref[...]) * nw).astype(bf16)
        hh_ref[F32_SUBLANES:, :] = (_rms(xn_ref[...]) * nw).astype(bf16)

    h = h_ref[...]
    hh = hh_ref[...]
    row = lax.broadcasted_iota(jnp.int32, (tm, 1), 0)
    t0 = i * tm
    at_seq_start = (t0 % S) == 0
    at_seq_end = ((t0 + tm) % S) == 0
    for c0 in range(0, o_ref.shape[1], MXU_WIDTH):
        cols = slice(c0, c0 + MXU_WIDTH)
        wg = wg_ref[:, cols]
        a = _dot(h, wa_ref[:, cols])
        g = _dot(h, wg)
        g_halo = _dot(hh, wg)
        prev = jnp.where(at_seq_start, 0.0, g_halo[F32_SUBLANES - 1:F32_SUBLANES, :])
        nxt = jnp.where(at_seq_end, 0.0, g_halo[F32_SUBLANES:F32_SUBLANES + 1, :])
        g_up = jnp.where(row == 0, prev, pltpu.roll(g, 1, 0))
        g_dn = jnp.where(row == tm - 1, nxt, pltpu.roll(g, tm - 1, 0))
        conv = g_up * cw_ref[0:1, cols] + g * cw_ref[1:2, cols] + g_dn * cw_ref[2:3, cols] + cb_ref[:, cols]
        o_ref[:, cols] = (jax.nn.gelu(conv) * a).astype(o_ref.dtype)


def ffn_gate(x, nw, w_up, conv_w, conv_b, *, S, tm=1024, tf=512):
    T, D = x.shape
    nf = D_FF // tf
    rb = tm // F32_SUBLANES
    last_rb = T // F32_SUBLANES - 1
    assert S % tm == 0
    return pl.pallas_call(
        functools.partial(_ffn_gate_kernel, S=S, tm=tm),
        grid=(T // tm, nf),
        in_specs=[pl.BlockSpec((tm, D), lambda i, j: (i, 0)),
                  pl.BlockSpec((F32_SUBLANES, D), lambda i, j: (jnp.maximum(i * rb - 1, 0), 0)),
                  pl.BlockSpec((F32_SUBLANES, D), lambda i, j: (jnp.minimum((i + 1) * rb, last_rb), 0)),
                  pl.BlockSpec((1, D), lambda i, j: (0, 0)),
                  pl.BlockSpec((D, tf), lambda i, j: (0, j)),
                  pl.BlockSpec((D, tf), lambda i, j: (0, nf + j)),
                  pl.BlockSpec((3, tf), lambda i, j: (0, j)),
                  pl.BlockSpec((1, tf), lambda i, j: (0, j))],
        out_specs=pl.BlockSpec((tm, tf), lambda i, j: (i, j)),
        out_shape=jax.ShapeDtypeStruct((T, D_FF), bf16),
        scratch_shapes=[pltpu.VMEM((tm, D), bf16), pltpu.VMEM((2 * F32_SUBLANES, D), bf16)],
        compiler_params=pltpu.CompilerParams(
            dimension_semantics=("parallel", "arbitrary"), vmem_limit_bytes=VMEM_LIMIT),
        name="ffn_gate",
    )(x, x, x, nw.reshape(1, D), w_up, w_up, conv_w, conv_b.reshape(1, D_FF))


def _ffn_down_kernel(x_ref, g_ref, wd_ref, fw_ref, o_ref, *, final_norm):
    k = pl.program_id(1)

    @pl.when(k == 0)
    def _():
        o_ref[...] = x_ref[...]

    o_ref[...] += _dot(g_ref[...], wd_ref[...])

    if final_norm:
        @pl.when(k == pl.num_programs(1) - 1)
        def _():
            x = o_ref[...]
            y = x * lax.rsqrt(jnp.mean(x * x, axis=-1, keepdims=True) + NORM_EPS)
            o_ref[...] = y * fw_ref[...]


def ffn_down(x, gated, w_down, final_w, *, final_norm, tm=512, tk=1408):
    T, D = x.shape
    nk = D_FF // tk
    return pl.pallas_call(
        functools.partial(_ffn_down_kernel, final_norm=final_norm),
        grid=(T // tm, nk),
        in_specs=[pl.BlockSpec((tm, D), lambda i, k: (i, 0)),
                  pl.BlockSpec((tm, tk), lambda i, k: (i, k)),
                  pl.BlockSpec((tk, D), lambda i, k: (k, 0)),
                  pl.BlockSpec((1, D), lambda i, k: (0, 0))],
        out_specs=pl.BlockSpec((tm, D), lambda i, k: (i, 0)),
        out_shape=jax.ShapeDtypeStruct((T, D), f32),
        compiler_params=pltpu.CompilerParams(
            dimension_semantics=("parallel", "arbitrary"), vmem_limit_bytes=VMEM_LIMIT),
        name="ffn_down",
    )(x, gated, w_down, final_w.reshape(1, D))


def _prepare_layer(l, norm1_w, w_in, ret_decay_fwd, ret_decay_bwd, ret_norm_w, na_rpb,
                   lq1, lk1, lq2, lk2, diff_norm_w, w_out, norm2_w, ffn_w_up, ffn_conv_w,
                   ffn_conv_b, ffn_w_down):
    r0, n0, d0 = 0, 4 * RET_W, 4 * RET_W + 3 * NA_W
    w = w_in[l]
    w_perm = jnp.concatenate([w[:, n0:d0], w[:, d0:], w[:, r0:n0]], axis=1).astype(bf16)
    lam_init = 0.8 - 0.6 * math.exp(-0.3 * l)
    lam = (jnp.exp(jnp.sum(lq1[l].astype(f32) * lk1[l].astype(f32)))
           - jnp.exp(jnp.sum(lq2[l].astype(f32) * lk2[l].astype(f32))) + lam_init)
    return dict(
        norm1_w=norm1_w[l].astype(f32), w_in=w_perm,
        lgf=jax.nn.log_sigmoid(ret_decay_fwd[l].astype(f32)),
        lgb=jax.nn.log_sigmoid(ret_decay_bwd[l].astype(f32)),
        ret_norm_w=ret_norm_w[l].astype(f32),
        na_bias=na_bias_table(na_rpb[l]),
        lam=lam.reshape(1), lam_init=lam_init, diff_norm_w=diff_norm_w[l].astype(f32),
        w_out=w_out[l].astype(bf16), norm2_w=norm2_w[l].astype(f32),
        w_up=ffn_w_up[l].astype(bf16), conv_w=ffn_conv_w[l].astype(f32),
        conv_b=ffn_conv_b[l].astype(f32), w_down=ffn_w_down[l].astype(bf16))


def _trunk(x, layers, final_norm_w, slope_vecs):
    B, S, D = x.shape
    x = x.reshape(B * S, D)
    for l, p in enumerate(layers):
        proj = norm_matmul(x, p["norm1_w"], p["w_in"])
        ret = retention(proj, p["lgf"], p["lgb"], p["ret_norm_w"], B=B, S=S)
        na = neighbourhood_attention(proj, p["na_bias"], B=B, S=S)
        diff = diff_attention(proj, slope_vecs, p["lam"], p["diff_norm_w"], B=B, S=S, lam_init=p["lam_init"])
        x = out_proj(x, ret, na, diff, p["w_out"])
        gated = ffn_gate(x, p["norm2_w"], p["w_up"], p["conv_w"], p["conv_b"], S=S)
        x = ffn_down(x, gated, p["w_down"], final_norm_w.astype(f32), final_norm=(l == len(layers) - 1))
    return x.reshape(B, S, D)


def kernel(x_prompt, x_sample, norm1_w, w_in, ret_decay_fwd, ret_decay_bwd, ret_norm_w, na_rpb, diff_lambda_q1, diff_lambda_k1, diff_lambda_q2, diff_lambda_k2, diff_norm_w, w_out, norm2_w, ffn_w_up, ffn_conv_w, ffn_conv_b, ffn_w_down, final_norm_w):
    layers = [_prepare_layer(l, norm1_w, w_in, ret_decay_fwd, ret_decay_bwd, ret_norm_w, na_rpb,
                             diff_lambda_q1, diff_lambda_k1, diff_lambda_q2, diff_lambda_k2,
                             diff_norm_w, w_out, norm2_w, ffn_w_up, ffn_conv_w, ffn_conv_b, ffn_w_down)
              for l in range(w_in.shape[0])]
    slopes = 2.0 ** (-8.0 * (jnp.arange(DIFF_HEADS, dtype=f32) + 1.0) / DIFF_HEADS)
    slope_vecs = diff_slope_vectors(slopes)
    y_prompt = _trunk(x_prompt, layers, final_norm_w, slope_vecs)
    y_sample = _trunk(x_sample, layers, final_norm_w, slope_vecs)
    return (y_prompt, y_sample)
```

```python
import functools
import math

import jax
import jax.numpy as jnp
import numpy as np
from jax import lax
from jax.experimental import pallas as pl
from jax.experimental.pallas import tpu as pltpu

D_MODEL = 2048
DEPTH = 2
HEAD_DIM = 64
RET_HEADS = 8
RET_W = RET_HEADS * HEAD_DIM
NA_HEADS = 12
NA_W = NA_HEADS * HEAD_DIM
GRID_W = 64
NA_KH = 8
NA_KW = 16
DIFF_HEADS = 6
DIFF_V_DIM = 2 * HEAD_DIM
DIFF_W = DIFF_HEADS * DIFF_V_DIM
D_IN = 4 * RET_W + 3 * NA_W + 3 * DIFF_W
D_FF = 5632
NORM_EPS = 1e-6
NEG_INF = -1e30

LANES = 128
F32_SUBLANES = 8
MXU_WIDTH = 256
RET_CHUNK = 256
RET_UNROLL = 4
NA_ROWS_PER_STEP = 4
DIFF_TQ = 256
DIFF_POS_RADIX = 64
DIFF_SLOPE_PARTS = 3
LOG2E = 1.4426950408889634
VMEM_LIMIT = 56 * 1024 * 1024

NA_COL0 = 0
DIFF_COL0 = 3 * NA_W
RET_COL0 = 3 * NA_W + 3 * DIFF_W

f32 = jnp.float32
bf16 = jnp.bfloat16


def _nt_dot(a, b):
    return lax.dot_general(a, b, (((1,), (1,)), ((), ())), preferred_element_type=f32)


def _tn_dot(a, b):
    return lax.dot_general(a, b, (((0,), (0,)), ((), ())), preferred_element_type=f32)


def _dot(a, b):
    return jnp.dot(a, b, preferred_element_type=f32)


def _norm_matmul_kernel(x_ref, nw_ref, w_ref, o_ref, h_ref):
    @pl.when(pl.program_id(1) == 0)
    def _():
        x = x_ref[...]
        y = x * lax.rsqrt(jnp.mean(x * x, axis=-1, keepdims=True) + NORM_EPS)
        h_ref[...] = (y * nw_ref[...]).astype(bf16)

    o_ref[...] = _dot(h_ref[...], w_ref[...]).astype(o_ref.dtype)


def norm_matmul(x, nw, w, *, tm=1024, tn=512):
    T, D = x.shape
    N = w.shape[1]
    return pl.pallas_call(
        _norm_matmul_kernel,
        grid=(T // tm, N // tn),
        in_specs=[
            pl.BlockSpec((tm, D), lambda i, j: (i, 0)),
            pl.BlockSpec((1, D), lambda i, j: (0, 0)),
            pl.BlockSpec((D, tn), lambda i, j: (0, j)),
        ],
        out_specs=pl.BlockSpec((tm, tn), lambda i, j: (i, j)),
        out_shape=jax.ShapeDtypeStruct((T, N), bf16),
        scratch_shapes=[pltpu.VMEM((tm, D), bf16)],
        compiler_params=pltpu.CompilerParams(
            dimension_semantics=("parallel", "arbitrary"), vmem_limit_bytes=VMEM_LIMIT),
        name="norm_matmul",
    )(x, nw.reshape(1, D), w)


def _retention_kernel(lgf_ref, lgb_ref, q_ref, k_ref, v_ref, g_ref, nw_ref, o_ref, acc_ref, *, S):
    C = RET_CHUNK
    n = S // C
    pair = pl.program_id(1)
    scale = HEAD_DIM ** -0.5

    lane = lax.broadcasted_iota(jnp.int32, (1, LANES), 1)
    lo = lane < HEAD_DIM
    lgf = jnp.where(lo, lgf_ref[2 * pair], lgf_ref[2 * pair + 1])
    lgb = jnp.where(lo, lgb_ref[2 * pair], lgb_ref[2 * pair + 1])

    pos = lax.broadcasted_iota(jnp.int32, (C, 1), 0).astype(f32)
    qdec_f = jnp.exp(lgf * (pos + 1.0))
    kdec_f = jnp.exp(lgf * (C - 1.0 - pos)) * scale
    cdec_f = jnp.exp(lgf * float(C))
    qdec_b = jnp.exp(lgb * (C - pos))
    kdec_b = jnp.exp(lgb * pos) * scale
    cdec_b = jnp.exp(lgb * float(C))

    ri = lax.broadcasted_iota(jnp.int32, (C, C), 0)
    ci = lax.broadcasted_iota(jnp.int32, (C, C), 1)
    delta = (ri - ci).astype(f32)

    def decay_mat(h):
        fwd = jnp.exp(lgf_ref[2 * pair + h] * jnp.maximum(delta, 0.0))
        bwd = jnp.exp(lgb_ref[2 * pair + h] * jnp.maximum(-delta, 0.0))
        return jnp.where(delta >= 0, fwd, bwd) * scale

    dmat0 = decay_mat(0)
    dmat1 = decay_mat(1)

    r2 = lax.broadcasted_iota(jnp.int32, (LANES, LANES), 0) // HEAD_DIM
    c2 = lax.broadcasted_iota(jnp.int32, (LANES, LANES), 1) // HEAD_DIM
    same_head = r2 == c2
    avg = jnp.where(same_head, 1.0 / HEAD_DIM, 0.0).astype(bf16)

    def chunk(ref, c):
        return ref[pl.ds(pl.multiple_of(c * C, C), C), :]

    def fwd_body(c, state):
        qc, kc, vc = chunk(q_ref, c), chunk(k_ref, c), chunk(v_ref, c)
        zero = jnp.zeros_like(qc)
        in0 = _nt_dot(jnp.where(lo, qc, zero), kc)
        in1 = _nt_dot(jnp.where(lo, zero, qc), kc)
        w = jnp.concatenate([(in0 * dmat0).astype(bf16), (in1 * dmat1).astype(bf16)], axis=1)
        vv = jnp.concatenate([jnp.where(lo, vc, zero), jnp.where(lo, zero, vc)], axis=0)
        intra = _dot(w, vv)
        qd = (qc.astype(f32) * qdec_f).astype(bf16)
        cross = _dot(qd, state.astype(bf16))
        acc_ref[pl.ds(pl.multiple_of(c * C, C), C), :] = intra + cross
        kd = (kc.astype(f32) * kdec_f).astype(bf16)
        upd = _tn_dot(kd, vc)
        return state * cdec_f + jnp.where(same_head, upd, 0.0)

    lax.fori_loop(0, n, fwd_body, jnp.zeros((LANES, LANES), f32), unroll=RET_UNROLL)

    def lane_mean(y):
        hi = y.astype(bf16)
        lo_part = (y - hi.astype(f32)).astype(bf16)
        return _dot(hi, avg) + _dot(lo_part, avg)

    def bwd_body(t, state):
        c = n - 1 - t
        qc, kc, vc = chunk(q_ref, c), chunk(k_ref, c), chunk(v_ref, c)
        qd = (qc.astype(f32) * qdec_b).astype(bf16)
        y = chunk(acc_ref, c) + _dot(qd, state.astype(bf16))
        y = y - lane_mean(y)
        y = y * lax.rsqrt(lane_mean(y * y) + NORM_EPS)
        gc = chunk(g_ref, c).astype(f32)
        out = jax.nn.silu(gc) * (y * nw_ref[...])
        o_ref[pl.ds(pl.multiple_of(c * C, C), C), :] = out.astype(o_ref.dtype)
        kd = (kc.astype(f32) * kdec_b).astype(bf16)
        upd = _tn_dot(kd, vc)
        return state * cdec_b + jnp.where(same_head, upd, 0.0)

    lax.fori_loop(0, n, bwd_body, jnp.zeros((LANES, LANES), f32), unroll=RET_UNROLL)


def retention(proj, lgf, lgb, norm_w, *, B, S):
    T = B * S
    blk0 = RET_COL0 // LANES
    npair = RET_HEADS // 2

    def col(seg):
        return lambda b, p: (b, blk0 + seg * npair + p)

    smem = pl.BlockSpec(memory_space=pltpu.SMEM)
    return pl.pallas_call(
        functools.partial(_retention_kernel, S=S),
        grid=(B, npair),
        in_specs=[smem, smem,
                  pl.BlockSpec((S, LANES), col(0)),
                  pl.BlockSpec((S, LANES), col(1)),
                  pl.BlockSpec((S, LANES), col(2)),
                  pl.BlockSpec((S, LANES), col(3)),
                  pl.BlockSpec((1, LANES), lambda b, p: (0, p))],
        out_specs=pl.BlockSpec((S, LANES), lambda b, p: (b, p)),
        out_shape=jax.ShapeDtypeStruct((T, RET_W), bf16),
        scratch_shapes=[pltpu.VMEM((S, LANES), f32)],
        compiler_params=pltpu.CompilerParams(
            dimension_semantics=("parallel", "parallel"), vmem_limit_bytes=VMEM_LIMIT),
        name="retention",
    )(lgf, lgb, proj, proj, proj, proj, norm_w.reshape(1, RET_W))


def _na_window_start(r, rows):
    return jnp.clip(r - NA_KH // 2, 0, rows - NA_KH)


def _na_kernel(q_ref, k_ref, v_ref, bias_ref, o_ref, *, rows):
    g = pl.program_id(1)
    nwin = NA_KH * GRID_W
    lane = lax.broadcasted_iota(jnp.int32, (1, LANES), 1)
    lo = lane < HEAD_DIM
    scale = HEAD_DIM ** -0.5
    npair = NA_HEADS // 2
    for j in range(NA_ROWS_PER_STEP):
        r = g * NA_ROWS_PER_STEP + j
        start = pl.multiple_of(_na_window_start(r, rows) * GRID_W, GRID_W)
        tok = slice(j * GRID_W, (j + 1) * GRID_W)
        parts = []
        for p in range(npair):
            cols = slice(p * LANES, (p + 1) * LANES)
            qp = (q_ref[tok, cols].astype(f32) * scale).astype(bf16)
            kp = k_ref[pl.ds(start, nwin), cols]
            zero = jnp.zeros_like(qp)
            parts.append(_nt_dot(jnp.where(lo, qp, zero), kp))
            parts.append(_nt_dot(jnp.where(lo, zero, qp), kp))
        d0 = NA_KH - 1 - (r - _na_window_start(r, rows))
        bias = jnp.concatenate(
            [jnp.concatenate([bias_ref[h, d0 + w] for w in range(0, NA_KH, 2)], axis=1)
             for h in range(NA_HEADS)], axis=0)
        s = jnp.concatenate(parts, axis=0) + bias
        e = jnp.exp(s - jnp.max(s, axis=-1, keepdims=True))
        inv = 1.0 / jnp.sum(e, axis=-1, keepdims=True)
        eb = e.astype(bf16)
        for p in range(npair):
            cols = slice(p * LANES, (p + 1) * LANES)
            vp = v_ref[pl.ds(start, nwin), cols]
            r0, r1, r2 = 2 * p * GRID_W, (2 * p + 1) * GRID_W, (2 * p + 2) * GRID_W
            even = _dot(eb[r0:r1], vp) * inv[r0:r1]
            odd = _dot(eb[r1:r2], vp) * inv[r1:r2]
            o_ref[tok, cols] = jnp.where(lo, even, odd).astype(o_ref.dtype)


def neighbourhood_attention(proj, bias_table, *, B, S):
    T = B * S
    rows = S // GRID_W
    nwin = NA_KH * GRID_W
    R = NA_ROWS_PER_STEP
    steps = rows // R
    assert rows % R == 0 and rows >= NA_KH

    return pl.pallas_call(
        functools.partial(_na_kernel, rows=rows),
        grid=(B, steps),
        in_specs=[pl.BlockSpec((R * GRID_W, NA_W), lambda b, g: (b * steps + g, 0)),
                  pl.BlockSpec((S, NA_W), lambda b, g: (b, 1)),
                  pl.BlockSpec((S, NA_W), lambda b, g: (b, 2)),
                  pl.BlockSpec(bias_table.shape, lambda b, g: (0, 0, 0, 0))],
        out_specs=pl.BlockSpec((R * GRID_W, NA_W), lambda b, g: (b * steps + g, 0)),
        out_shape=jax.ShapeDtypeStruct((T, NA_W), bf16),
        compiler_params=pltpu.CompilerParams(
            dimension_semantics=("parallel", "arbitrary"), vmem_limit_bytes=VMEM_LIMIT),
        name="neighbourhood_attention",
    )(proj, proj, proj, bias_table)


def na_bias_table(rpb):
    qcol = np.arange(GRID_W)[:, None]
    kcol = np.arange(GRID_W)[None, :]
    cs = np.clip(qcol - NA_KW // 2, 0, GRID_W - NA_KW)
    in_win = (kcol >= cs) & (kcol < cs + NA_KW)
    dc = np.clip(kcol - qcol + NA_KW - 1, 0, 2 * NA_KW - 2)
    blocks = jnp.where(jnp.asarray(in_win), rpb.astype(f32)[:, :, dc], NEG_INF)
    return jnp.concatenate([blocks[:, :-1], blocks[:, 1:]], axis=-1)


def _diff_position_features(pos, lane_in_half, first):
    f = lane_in_half - first
    lo = jnp.bitwise_and(pos, DIFF_POS_RADIX - 1)
    hi = pos - lo
    val = jnp.where(jnp.bitwise_and(f, 1) == 0, hi, lo)
    return jnp.where((f >= 0) & (f < 2 * DIFF_SLOPE_PARTS), val, 0).astype(f32)


def _diff_kernel(lam_ref, q_ref, qnext_ref, k_ref, v_ref, cvec_ref, nw_ref, o_ref,
                 kaug_ref, vaug_ref, qv_ref, s_even_ref, s_odd_ref, m_even_ref, m_odd_ref, *, S, out_scale):
    tq = DIFF_TQ
    nkt = S // tq
    qi = pl.program_id(2)
    lane = lax.broadcasted_iota(jnp.int32, (1, LANES), 1)
    lo = lane < HEAD_DIM
    lane_in_half = jnp.bitwise_and(lane, HEAD_DIM - 1)
    cq = cvec_ref[0:1, :]
    ck = cvec_ref[1:2, :]

    def key_rows(tile, r):
        t = jnp.bitwise_and(tile + r, nkt - 1)
        return pl.ds(pl.multiple_of(t * tq, tq), tq)

    def halves_max(x):
        return jnp.maximum(x[:, :LANES], x[:, LANES:])

    def pipeline(q_blk_ref, nxt_tile, nxt_s_ref, nxt_m_ref, cur_tile, cur_s_ref, cur_m_ref):
        qs = (q_blk_ref[...].astype(f32) * (HEAD_DIM ** -0.5 * LOG2E)).astype(bf16)
        qpos = nxt_tile * tq + lax.broadcasted_iota(jnp.int32, (tq, 1), 0)
        fq = _diff_position_features(qpos, lane_in_half, 0) + cq
        zero = jnp.zeros_like(qs)
        for v, f in enumerate((fq.astype(bf16), zero, (-fq).astype(bf16))):
            qv_ref[v] = jnp.where(lo, qs, f)
            qv_ref[3 + v] = jnp.where(lo, f, qs)
        d = (lax.broadcasted_iota(jnp.int32, (tq, tq), 0)
             - lax.broadcasted_iota(jnp.int32, (tq, tq), 1)).astype(f32)
        diag_bias = -cvec_ref[2:3, 0:1] * jnp.abs(d)

        if cur_s_ref is not None:
            m_cur = [jnp.max(cur_m_ref[c], axis=1, keepdims=True) for c in range(2)]
            acc = [None, None]
        m_acc = [None, None]
        for r in range(nkt):
            if cur_s_ref is not None:
                v_ones = vaug_ref[key_rows(cur_tile, r), :]
                for c in range(2):
                    e = jnp.exp2(cur_s_ref[c, r] - m_cur[c]).astype(bf16)
                    pv = _dot(e, v_ones)
                    acc[c] = pv if acc[c] is None else acc[c] + pv
            rows = key_rows(nxt_tile, r)
            variant = 1 if r == 0 else jnp.where(nxt_tile + r >= nkt, 0, 2)
            for c in range(2):
                s = _nt_dot(qv_ref[3 * c + variant], kaug_ref[c, rows, :])
                if r == 0:
                    s = s + diag_bias
                nxt_s_ref[c, r] = s
                m_acc[c] = halves_max(s) if r == 0 else jnp.maximum(m_acc[c], halves_max(s))
        for c in range(2):
            nxt_m_ref[c] = m_acc[c]
        if cur_s_ref is None:
            return

        r1 = 1.0 / acc[0][:, DIFF_V_DIM:DIFF_V_DIM + 1]
        r2 = lam_ref[0] / acc[1][:, DIFF_V_DIM:DIFF_V_DIM + 1]
        out = acc[0][:, :DIFF_V_DIM] * r1 - acc[1][:, :DIFF_V_DIM] * r2
        y = out * lax.rsqrt(jnp.mean(out * out, axis=-1, keepdims=True) + NORM_EPS)
        o_ref[...] = (y * nw_ref[...] * out_scale).astype(o_ref.dtype)

    @pl.when(qi == 0)
    def _():
        kpos = lax.broadcasted_iota(jnp.int32, (S, 1), 0)
        fk = (_diff_position_features(kpos, lane_in_half, 2 * DIFF_SLOPE_PARTS) + ck).astype(bf16)
        k = k_ref[...]
        kaug_ref[0] = jnp.where(lo, k, fk)
        kaug_ref[1] = jnp.where(lo, fk, k)
        vaug_ref[:, 0:DIFF_V_DIM] = v_ref[...]
        vaug_ref[:, DIFF_V_DIM:] = jnp.broadcast_to(jnp.where(lane == 0, 1.0, 0.0).astype(bf16), (S, LANES))
        pipeline(q_ref, qi, s_even_ref, m_even_ref, None, None, None)

    @pl.when(qi % 2 == 0)
    def _():
        pipeline(qnext_ref, qi + 1, s_odd_ref, m_odd_ref, qi, s_even_ref, m_even_ref)

    @pl.when(qi % 2 == 1)
    def _():
        pipeline(qnext_ref, qi + 1, s_even_ref, m_even_ref, qi, s_odd_ref, m_odd_ref)


def diff_slope_vectors(slopes):
    c = slopes.astype(f32) * LOG2E
    parts = []
    rest = c
    for _ in range(DIFF_SLOPE_PARTS):
        p = rest.astype(bf16).astype(f32)
        parts.append(p)
        rest = rest - p
    parts = jnp.stack(parts, axis=1)
    lane_in_half = np.arange(LANES) % HEAD_DIM
    n = 2 * DIFF_SLOPE_PARTS
    q_sel = np.where((lane_in_half >= n) & (lane_in_half < 2 * n), (lane_in_half - n) // 2, -1)
    k_sel = np.where(lane_in_half < n, lane_in_half // 2, -1)
    zero = jnp.zeros((slopes.shape[0], 1), f32)
    padded = jnp.concatenate([parts, zero], axis=1)
    cq = padded[:, q_sel]
    ck = -padded[:, k_sel]
    return jnp.stack([cq, ck, jnp.broadcast_to(c[:, None], cq.shape)], axis=1)


def diff_attention(proj, cvec, lam, norm_w, *, B, S, lam_init):
    T = B * S
    tq = DIFF_TQ
    nq = S // tq
    assert S <= DIFF_POS_RADIX * DIFF_POS_RADIX and nq & (nq - 1) == 0
    blk0 = DIFF_COL0 // LANES
    smem = pl.BlockSpec(memory_space=pltpu.SMEM)
    return pl.pallas_call(
        functools.partial(_diff_kernel, S=S, out_scale=1.0 - lam_init),
        grid=(B, DIFF_HEADS, nq),
        in_specs=[smem,
                  pl.BlockSpec((tq, LANES), lambda b, h, i: (b * nq + i, blk0 + h)),
                  pl.BlockSpec((tq, LANES), lambda b, h, i: (b * nq + jnp.minimum(i + 1, nq - 1), blk0 + h)),
                  pl.BlockSpec((S, LANES), lambda b, h, i: (b, blk0 + DIFF_HEADS + h)),
                  pl.BlockSpec((S, LANES), lambda b, h, i: (b, blk0 + 2 * DIFF_HEADS + h)),
                  pl.BlockSpec((None, 3, LANES), lambda b, h, i: (h, 0, 0)),
                  pl.BlockSpec((1, LANES), lambda b, h, i: (0, h))],
        out_specs=pl.BlockSpec((tq, LANES), lambda b, h, i: (b * nq + i, h)),
        out_shape=jax.ShapeDtypeStruct((T, DIFF_W), bf16),
        scratch_shapes=[pltpu.VMEM((2, S, LANES), bf16),
                        pltpu.VMEM((S, 2 * LANES), bf16),
                        pltpu.VMEM((6, tq, LANES), bf16),
                        pltpu.VMEM((2, nq, tq, tq), f32),
                        pltpu.VMEM((2, nq, tq, tq), f32),
                        pltpu.VMEM((2, tq, LANES), f32),
                        pltpu.VMEM((2, tq, LANES), f32)],
        compiler_params=pltpu.CompilerParams(
            dimension_semantics=("parallel", "parallel", "arbitrary"), vmem_limit_bytes=VMEM_LIMIT),
        name="diff_attention",
    )(lam, proj, proj, proj, proj, cvec, norm_w.reshape(1, DIFF_W))


def _out_proj_kernel(x_ref, r_ref, n_ref, d_ref, w_ref, o_ref):
    acc = _dot(r_ref[...], w_ref[0:RET_W, :])
    acc += _dot(n_ref[...], w_ref[RET_W:RET_W + NA_W, :])
    acc += _dot(d_ref[...], w_ref[RET_W + NA_W:, :])
    o_ref[...] = x_ref[...] + acc


def out_proj(x, ret, na, diff, w_out, *, tm=512):
    T, D = x.shape
    row = lambda i: (i, 0)
    return pl.pallas_call(
        _out_proj_kernel,
        grid=(T // tm,),
        in_specs=[pl.BlockSpec((tm, D), row),
                  pl.BlockSpec((tm, RET_W), row),
                  pl.BlockSpec((tm, NA_W), row),
                  pl.BlockSpec((tm, DIFF_W), row),
                  pl.BlockSpec((D, D), lambda i: (0, 0))],
        out_specs=pl.BlockSpec((tm, D), row),
        out_shape=jax.ShapeDtypeStruct((T, D), f32),
        compiler_params=pltpu.CompilerParams(
            dimension_semantics=("parallel",), vmem_limit_bytes=VMEM_LIMIT),
        name="out_proj",
    )(x, ret, na, diff, w_out)


def _rms(x):
    return x * lax.rsqrt(jnp.mean(x * x, axis=-1, keepdims=True) + NORM_EPS)


def _ffn_gate_kernel(x_ref, xp_ref, xn_ref, nw_ref, wa_ref, wg_ref, cw_ref, cb_ref, o_ref, h_ref, hh_ref,
                     *, S, tm):
    i = pl.program_id(0)

    @pl.when(pl.program_id(1) == 0)
    def _():
        nw = nw_ref[...]
        h_ref[...] = (_rms(x_ref[...]) * nw).astype(bf16)
        hh_ref[0:F32_SUBLANES, :] = (_rms(xp_ref[...]) * nw).astype(bf16)
        hh_ref[F32_SUBLANES:, :] = (_rms(xn_ref[...]) * nw).astype(bf16)

    h = h_ref[...]
    hh = hh_ref[...]
    row = lax.broadcasted_iota(jnp.int32, (tm, 1), 0)
    t0 = i * tm
    at_seq_start = (t0 % S) == 0
    at_seq_end = ((t0 + tm) % S) == 0
    for c0 in range(0, o_ref.shape[1], MXU_WIDTH):
        cols = slice(c0, c0 + MXU_WIDTH)
        wg = wg_ref[:, cols]
        a = _dot(h, wa_ref[:, cols])
        g = _dot(h, wg)
        g_halo = _dot(hh, wg)
        prev = jnp.where(at_seq_start, 0.0, g_halo[F32_SUBLANES - 1:F32_SUBLANES, :])
        nxt = jnp.where(at_seq_end, 0.0, g_halo[F32_SUBLANES:F32_SUBLANES + 1, :])
        g_up = jnp.where(row == 0, prev, pltpu.roll(g, 1, 0))
        g_dn = jnp.where(row == tm - 1, nxt, pltpu.roll(g, tm - 1, 0))
        conv = g_up * cw_ref[0:1, cols] + g * cw_ref[1:2, cols] + g_dn * cw_ref[2:3, cols] + cb_ref[:, cols]
        o_ref[:, cols] = (jax.nn.gelu(conv) * a).astype(o_ref.dtype)


def ffn_gate(x, nw, w_up, conv_w, conv_b, *, S, tm=1024, tf=512):
    T, D = x.shape
    nf = D_FF // tf
    rb = tm // F32_SUBLANES
    last_rb = T // F32_SUBLANES - 1
    assert S % tm == 0
    return pl.pallas_call(
        functools.partial(_ffn_gate_kernel, S=S, tm=tm),
        grid=(T // tm, nf),
        in_specs=[pl.BlockSpec((tm, D), lambda i, j: (i, 0)),
                  pl.BlockSpec((F32_SUBLANES, D), lambda i, j: (jnp.maximum(i * rb - 1, 0), 0)),
                  pl.BlockSpec((F32_SUBLANES, D), lambda i, j: (jnp.minimum((i + 1) * rb, last_rb), 0)),
                  pl.BlockSpec((1, D), lambda i, j: (0, 0)),
                  pl.BlockSpec((D, tf), lambda i, j: (0, j)),
                  pl.BlockSpec((D, tf), lambda i, j: (0, nf + j)),
                  pl.BlockSpec((3, tf), lambda i, j: (0, j)),
                  pl.BlockSpec((1, tf), lambda i, j: (0, j))],
        out_specs=pl.BlockSpec((tm, tf), lambda i, j: (i, j)),
        out_shape=jax.ShapeDtypeStruct((T, D_FF), bf16),
        scratch_shapes=[pltpu.VMEM((tm, D), bf16), pltpu.VMEM((2 * F32_SUBLANES, D), bf16)],
        compiler_params=pltpu.CompilerParams(
            dimension_semantics=("parallel", "arbitrary"), vmem_limit_bytes=VMEM_LIMIT),
        name="ffn_gate",
    )(x, x, x, nw.reshape(1, D), w_up, w_up, conv_w, conv_b.reshape(1, D_FF))


def _ffn_down_kernel(x_ref, g_ref, wd_ref, fw_ref, o_ref, *, final_norm):
    k = pl.program_id(1)

    @pl.when(k == 0)
    def _():
        o_ref[...] = x_ref[...]

    o_ref[...] += _dot(g_ref[...], wd_ref[...])

    if final_norm:
        @pl.when(k == pl.num_programs(1) - 1)
        def _():
            x = o_ref[...]
            y = x * lax.rsqrt(jnp.mean(x * x, axis=-1, keepdims=True) + NORM_EPS)
            o_ref[...] = y * fw_ref[...]


def ffn_down(x, gated, w_down, final_w, *, final_norm, tm=512, tk=1408):
    T, D = x.shape
    nk = D_FF // tk
    return pl.pallas_call(
        functools.partial(_ffn_down_kernel, final_norm=final_norm),
        grid=(T // tm, nk),
        in_specs=[pl.BlockSpec((tm, D), lambda i, k: (i, 0)),
                  pl.BlockSpec((tm, tk), lambda i, k: (i, k)),
                  pl.BlockSpec((tk, D), lambda i, k: (k, 0)),
                  pl.BlockSpec((1, D), lambda i, k: (0, 0))],
        out_specs=pl.BlockSpec((tm, D), lambda i, k: (i, 0)),
        out_shape=jax.ShapeDtypeStruct((T, D), f32),
        compiler_params=pltpu.CompilerParams(
            dimension_semantics=("parallel", "arbitrary"), vmem_limit_bytes=VMEM_LIMIT),
        name="ffn_down",
    )(x, gated, w_down, final_w.reshape(1, D))


def _prepare_layer(l, norm1_w, w_in, ret_decay_fwd, ret_decay_bwd, ret_norm_w, na_rpb,
                   lq1, lk1, lq2, lk2, diff_norm_w, w_out, norm2_w, ffn_w_up, ffn_conv_w,
                   ffn_conv_b, ffn_w_down):
    r0, n0, d0 = 0, 4 * RET_W, 4 * RET_W + 3 * NA_W
    w = w_in[l]
    w_perm = jnp.concatenate([w[:, n0:d0], w[:, d0:], w[:, r0:n0]], axis=1).astype(bf16)
    lam_init = 0.8 - 0.6 * math.exp(-0.3 * l)
    lam = (jnp.exp(jnp.sum(lq1[l].astype(f32) * lk1[l].astype(f32)))
           - jnp.exp(jnp.sum(lq2[l].astype(f32) * lk2[l].astype(f32))) + lam_init)
    return dict(
        norm1_w=norm1_w[l].astype(f32), w_in=w_perm,
        lgf=jax.nn.log_sigmoid(ret_decay_fwd[l].astype(f32)),
        lgb=jax.nn.log_sigmoid(ret_decay_bwd[l].astype(f32)),
        ret_norm_w=ret_norm_w[l].astype(f32),
        na_bias=na_bias_table(na_rpb[l]),
        lam=lam.reshape(1), lam_init=lam_init, diff_norm_w=diff_norm_w[l].astype(f32),
        w_out=w_out[l].astype(bf16), norm2_w=norm2_w[l].astype(f32),
        w_up=ffn_w_up[l].astype(bf16), conv_w=ffn_conv_w[l].astype(f32),
        conv_b=ffn_conv_b[l].astype(f32), w_down=ffn_w_down[l].astype(bf16))


def _trunk(x, layers, final_norm_w, slope_vecs):
    B, S, D = x.shape
    x = x.reshape(B * S, D)
    for l, p in enumerate(layers):
        proj = norm_matmul(x, p["norm1_w"], p["w_in"])
        ret = retention(proj, p["lgf"], p["lgb"], p["ret_norm_w"], B=B, S=S)
        na = neighbourhood_attention(proj, p["na_bias"], B=B, S=S)
        diff = diff_attention(proj, slope_vecs, p["lam"], p["diff_norm_w"], B=B, S=S, lam_init=p["lam_init"])
        x = out_proj(x, ret, na, diff, p["w_out"])
        gated = ffn_gate(x, p["norm2_w"], p["w_up"], p["conv_w"], p["conv_b"], S=S)
        x = ffn_down(x, gated, p["w_down"], final_norm_w.astype(f32), final_norm=(l == len(layers) - 1))
    return x.reshape(B, S, D)


def kernel(x_prompt, x_sample, norm1_w, w_in, ret_decay_fwd, ret_decay_bwd, ret_norm_w, na_rpb, diff_lambda_q1, diff_lambda_k1, diff_lambda_q2, diff_lambda_k2, diff_norm_w, w_out, norm2_w, ffn_w_up, ffn_conv_w, ffn_conv_b, ffn_w_down, final_norm_w):
    layers = [_prepare_layer(l, norm1_w, w_in, ret_decay_fwd, ret_decay_bwd, ret_norm_w, na_rpb,
                             diff_lambda_q1, diff_lambda_k1, diff_lambda_q2, diff_lambda_k2,
                             diff_norm_w, w_out, norm2_w, ffn_w_up, ffn_conv_w, ffn_conv_b, ffn_w_down)
              for l in range(w_in.shape[0])]
    slopes = 2.0 ** (-8.0 * (jnp.arange(DIFF_HEADS, dtype=f32) + 1.0) / DIFF_HEADS)
    slope_vecs = diff_slope_vectors(slopes)
    y_prompt = _trunk(x_prompt, layers, final_norm_w, slope_vecs)
    y_sample = _trunk(x_sample, layers, final_norm_w, slope_vecs)
    return (y_prompt, y_sample)
```

```python
import functools
import math

import jax
import jax.numpy as jnp
import numpy as np
from jax import lax
from jax.experimental import pallas as pl
from jax.experimental.pallas import tpu as pltpu

D_MODEL = 2048
DEPTH = 2
HEAD_DIM = 64
RET_HEADS = 8
RET_W = RET_HEADS * HEAD_DIM
NA_HEADS = 12
NA_W = NA_HEADS * HEAD_DIM
GRID_W = 64
NA_KH = 8
NA_KW = 16
DIFF_HEADS = 6
DIFF_V_DIM = 2 * HEAD_DIM
DIFF_W = DIFF_HEADS * DIFF_V_DIM
D_IN = 4 * RET_W + 3 * NA_W + 3 * DIFF_W
D_FF = 5632
NORM_EPS = 1e-6
NEG_INF = -1e30

LANES = 128
F32_SUBLANES = 8
MXU_WIDTH = 256
RET_CHUNK = 256
RET_UNROLL = 4
NA_ROWS_PER_STEP = 4
DIFF_TQ = 256
DIFF_POS_RADIX = 64
DIFF_SLOPE_PARTS = 3
LOG2E = 1.4426950408889634
VMEM_LIMIT = 56 * 1024 * 1024

NA_COL0 = 0
DIFF_COL0 = 3 * NA_W
RET_COL0 = 3 * NA_W + 3 * DIFF_W

f32 = jnp.float32
bf16 = jnp.bfloat16


def _nt_dot(a, b):
    return lax.dot_general(a, b, (((1,), (1,)), ((), ())), preferred_element_type=f32)


def _tn_dot(a, b):
    return lax.dot_general(a, b, (((0,), (0,)), ((), ())), preferred_element_type=f32)


def _dot(a, b):
    return jnp.dot(a, b, preferred_element_type=f32)


def _norm_matmul_kernel(x_ref, nw_ref, w_ref, o_ref, h_ref):
    @pl.when(pl.program_id(1) == 0)
    def _():
        x = x_ref[...]
        y = x * lax.rsqrt(jnp.mean(x * x, axis=-1, keepdims=True) + NORM_EPS)
        h_ref[...] = (y * nw_ref[...]).astype(bf16)

    o_ref[...] = _dot(h_ref[...], w_ref[...]).astype(o_ref.dtype)


def norm_matmul(x, nw, w, *, tm=512, tn=3328):
    T, D = x.shape
    N = w.shape[1]
    return pl.pallas_call(
        _norm_matmul_kernel,
        grid=(T // tm, N // tn),
        in_specs=[
            pl.BlockSpec((tm, D), lambda i, j: (i, 0)),
            pl.BlockSpec((1, D), lambda i, j: (0, 0)),
            pl.BlockSpec((D, tn), lambda i, j: (0, j)),
        ],
        out_specs=pl.BlockSpec((tm, tn), lambda i, j: (i, j)),
        out_shape=jax.ShapeDtypeStruct((T, N), bf16),
        scratch_shapes=[pltpu.VMEM((tm, D), bf16)],
        compiler_params=pltpu.CompilerParams(
            dimension_semantics=("parallel", "arbitrary"), vmem_limit_bytes=VMEM_LIMIT),
        name="norm_matmul",
    )(x, nw.reshape(1, D), w)


def _retention_kernel(lgf_ref, lgb_ref, q_ref, k_ref, v_ref, g_ref, nw_ref, o_ref, acc_ref, *, S):
    C = RET_CHUNK
    n = S // C
    pair = pl.program_id(1)
    scale = HEAD_DIM ** -0.5

    lane = lax.broadcasted_iota(jnp.int32, (1, LANES), 1)
    lo = lane < HEAD_DIM
    lgf = jnp.where(lo, lgf_ref[2 * pair], lgf_ref[2 * pair + 1])
    lgb = jnp.where(lo, lgb_ref[2 * pair], lgb_ref[2 * pair + 1])

    pos = lax.broadcasted_iota(jnp.int32, (C, 1), 0).astype(f32)
    qdec_f = jnp.exp(lgf * (pos + 1.0))
    kdec_f = jnp.exp(lgf * (C - 1.0 - pos)) * scale
    cdec_f = jnp.exp(lgf * float(C))
    qdec_b = jnp.exp(lgb * (C - pos))
    kdec_b = jnp.exp(lgb * pos) * scale
    cdec_b = jnp.exp(lgb * float(C))

    ri = lax.broadcasted_iota(jnp.int32, (C, C), 0)
    ci = lax.broadcasted_iota(jnp.int32, (C, C), 1)
    delta = (ri - ci).astype(f32)

    def decay_mat(h):
        fwd = jnp.exp(lgf_ref[2 * pair + h] * jnp.maximum(delta, 0.0))
        bwd = jnp.exp(lgb_ref[2 * pair + h] * jnp.maximum(-delta, 0.0))
        return jnp.where(delta >= 0, fwd, bwd) * scale

    dmat0 = decay_mat(0)
    dmat1 = decay_mat(1)

    r2 = lax.broadcasted_iota(jnp.int32, (LANES, LANES), 0) // HEAD_DIM
    c2 = lax.broadcasted_iota(jnp.int32, (LANES, LANES), 1) // HEAD_DIM
    same_head = r2 == c2
    avg = jnp.where(same_head, 1.0 / HEAD_DIM, 0.0).astype(bf16)

    def chunk(ref, c):
        return ref[pl.ds(pl.multiple_of(c * C, C), C), :]

    def fwd_body(c, state):
        qc, kc, vc = chunk(q_ref, c), chunk(k_ref, c), chunk(v_ref, c)
        zero = jnp.zeros_like(qc)
        in0 = _nt_dot(jnp.where(lo, qc, zero), kc)
        in1 = _nt_dot(jnp.where(lo, zero, qc), kc)
        w = jnp.concatenate([(in0 * dmat0).astype(bf16), (in1 * dmat1).astype(bf16)], axis=1)
        vv = jnp.concatenate([jnp.where(lo, vc, zero), jnp.where(lo, zero, vc)], axis=0)
        intra = _dot(w, vv)
        qd = (qc.astype(f32) * qdec_f).astype(bf16)
        cross = _dot(qd, state.astype(bf16))
        acc_ref[pl.ds(pl.multiple_of(c * C, C), C), :] = intra + cross
        kd = (kc.astype(f32) * kdec_f).astype(bf16)
        upd = _tn_dot(kd, vc)
        return state * cdec_f + jnp.where(same_head, upd, 0.0)

    lax.fori_loop(0, n, fwd_body, jnp.zeros((LANES, LANES), f32), unroll=RET_UNROLL)

    def lane_mean(y):
        hi = y.astype(bf16)
        lo_part = (y - hi.astype(f32)).astype(bf16)
        return _dot(hi, avg) + _dot(lo_part, avg)

    def bwd_body(t, state):
        c = n - 1 - t
        qc, kc, vc = chunk(q_ref, c), chunk(k_ref, c), chunk(v_ref, c)
        qd = (qc.astype(f32) * qdec_b).astype(bf16)
        y = chunk(acc_ref, c) + _dot(qd, state.astype(bf16))
        y = y - lane_mean(y)
        y = y * lax.rsqrt(lane_mean(y * y) + NORM_EPS)
        gc = chunk(g_ref, c).astype(f32)
        out = jax.nn.silu(gc) * (y * nw_ref[...])
        o_ref[pl.ds(pl.multiple_of(c * C, C), C), :] = out.astype(o_ref.dtype)
        kd = (kc.astype(f32) * kdec_b).astype(bf16)
        upd = _tn_dot(kd, vc)
        return state * cdec_b + jnp.where(same_head, upd, 0.0)

    lax.fori_loop(0, n, bwd_body, jnp.zeros((LANES, LANES), f32), unroll=RET_UNROLL)


def retention(proj, lgf, lgb, norm_w, *, B, S):
    T = B * S
    blk0 = RET_COL0 // LANES
    npair = RET_HEADS // 2

    def col(seg):
        return lambda b, p: (b, blk0 + seg * npair + p)

    smem = pl.BlockSpec(memory_space=pltpu.SMEM)
    return pl.pallas_call(
        functools.partial(_retention_kernel, S=S),
        grid=(B, npair),
        in_specs=[smem, smem,
                  pl.BlockSpec((S, LANES), col(0)),
                  pl.BlockSpec((S, LANES), col(1)),
                  pl.BlockSpec((S, LANES), col(2)),
                  pl.BlockSpec((S, LANES), col(3)),
                  pl.BlockSpec((1, LANES), lambda b, p: (0, p))],
        out_specs=pl.BlockSpec((S, LANES), lambda b, p: (b, p)),
        out_shape=jax.ShapeDtypeStruct((T, RET_W), bf16),
        scratch_shapes=[pltpu.VMEM((S, LANES), f32)],
        compiler_params=pltpu.CompilerParams(
            dimension_semantics=("parallel", "parallel"), vmem_limit_bytes=VMEM_LIMIT),
        name="retention",
    )(lgf, lgb, proj, proj, proj, proj, norm_w.reshape(1, RET_W))


def _na_window_start(r, rows):
    return jnp.clip(r - NA_KH // 2, 0, rows - NA_KH)


def _na_kernel(q_ref, k_ref, v_ref, bias_ref, o_ref, *, rows):
    g = pl.program_id(1)
    nwin = NA_KH * GRID_W
    lane = lax.broadcasted_iota(jnp.int32, (1, LANES), 1)
    lo = lane < HEAD_DIM
    scale = HEAD_DIM ** -0.5
    npair = NA_HEADS // 2
    for j in range(NA_ROWS_PER_STEP):
        r = g * NA_ROWS_PER_STEP + j
        start = pl.multiple_of(_na_window_start(r, rows) * GRID_W, GRID_W)
        tok = slice(j * GRID_W, (j + 1) * GRID_W)
        parts = []
        for p in range(npair):
            cols = slice(p * LANES, (p + 1) * LANES)
            qp = (q_ref[tok, cols].astype(f32) * scale).astype(bf16)
            kp = k_ref[pl.ds(start, nwin), cols]
            zero = jnp.zeros_like(qp)
            parts.append(_nt_dot(jnp.where(lo, qp, zero), kp))
            parts.append(_nt_dot(jnp.where(lo, zero, qp), kp))
        d0 = NA_KH - 1 - (r - _na_window_start(r, rows))
        bias = jnp.concatenate(
            [jnp.concatenate([bias_ref[h, d0 + w] for w in range(0, NA_KH, 2)], axis=1)
             for h in range(NA_HEADS)], axis=0)
        s = jnp.concatenate(parts, axis=0) + bias
        e = jnp.exp(s - jnp.max(s, axis=-1, keepdims=True))
        inv = 1.0 / jnp.sum(e, axis=-1, keepdims=True)
        eb = e.astype(bf16)
        for p in range(npair):
            cols = slice(p * LANES, (p + 1) * LANES)
            vp = v_ref[pl.ds(start, nwin), cols]
            r0, r1, r2 = 2 * p * GRID_W, (2 * p + 1) * GRID_W, (2 * p + 2) * GRID_W
            even = _dot(eb[r0:r1], vp) * inv[r0:r1]
            odd = _dot(eb[r1:r2], vp) * inv[r1:r2]
            o_ref[tok, cols] = jnp.where(lo, even, odd).astype(o_ref.dtype)


def neighbourhood_attention(proj, bias_table, *, B, S):
    T = B * S
    rows = S // GRID_W
    nwin = NA_KH * GRID_W
    R = NA_ROWS_PER_STEP
    steps = rows // R
    assert rows % R == 0 and rows >= NA_KH

    return pl.pallas_call(
        functools.partial(_na_kernel, rows=rows),
        grid=(B, steps),
        in_specs=[pl.BlockSpec((R * GRID_W, NA_W), lambda b, g: (b * steps + g, 0)),
                  pl.BlockSpec((S, NA_W), lambda b, g: (b, 1)),
                  pl.BlockSpec((S, NA_W), lambda b, g: (b, 2)),
                  pl.BlockSpec(bias_table.shape, lambda b, g: (0, 0, 0, 0))],
        out_specs=pl.BlockSpec((R * GRID_W, NA_W), lambda b, g: (b * steps + g, 0)),
        out_shape=jax.ShapeDtypeStruct((T, NA_W), bf16),
        compiler_params=pltpu.CompilerParams(
            dimension_semantics=("parallel", "arbitrary"), vmem_limit_bytes=VMEM_LIMIT),
        name="neighbourhood_attention",
    )(proj, proj, proj, bias_table)


def na_bias_table(rpb):
    qcol = np.arange(GRID_W)[:, None]
    kcol = np.arange(GRID_W)[None, :]
    cs = np.clip(qcol - NA_KW // 2, 0, GRID_W - NA_KW)
    in_win = (kcol >= cs) & (kcol < cs + NA_KW)
    dc = np.clip(kcol - qcol + NA_KW - 1, 0, 2 * NA_KW - 2)
    blocks = jnp.where(jnp.asarray(in_win), rpb.astype(f32)[:, :, dc], NEG_INF)
    return jnp.concatenate([blocks[:, :-1], blocks[:, 1:]], axis=-1)


def _diff_position_features(pos, lane_in_half, first):
    f = lane_in_half - first
    lo = jnp.bitwise_and(pos, DIFF_POS_RADIX - 1)
    hi = pos - lo
    val = jnp.where(jnp.bitwise_and(f, 1) == 0, hi, lo)
    return jnp.where((f >= 0) & (f < 2 * DIFF_SLOPE_PARTS), val, 0).astype(f32)


def _diff_kernel(lam_ref, q_ref, qnext_ref, k_ref, v_ref, cvec_ref, nw_ref, o_ref,
                 kaug_ref, vaug_ref, qv_ref, s_even_ref, s_odd_ref, m_even_ref, m_odd_ref, *, S, out_scale):
    tq = DIFF_TQ
    nkt = S // tq
    qi = pl.program_id(2)
    lane = lax.broadcasted_iota(jnp.int32, (1, LANES), 1)
    lo = lane < HEAD_DIM
    lane_in_half = jnp.bitwise_and(lane, HEAD_DIM - 1)
    cq = cvec_ref[0:1, :]
    ck = cvec_ref[1:2, :]

    def key_rows(tile, r):
        t = jnp.bitwise_and(tile + r, nkt - 1)
        return pl.ds(pl.multiple_of(t * tq, tq), tq)

    def halves_max(x):
        return jnp.maximum(x[:, :LANES], x[:, LANES:])

    def pipeline(q_blk_ref, nxt_tile, nxt_s_ref, nxt_m_ref, cur_tile, cur_s_ref, cur_m_ref, out_ref):
        qs = (q_blk_ref[...].astype(f32) * (HEAD_DIM ** -0.5 * LOG2E)).astype(bf16)
        qpos = nxt_tile * tq + lax.broadcasted_iota(jnp.int32, (tq, 1), 0)
        fq = _diff_position_features(qpos, lane_in_half, 0) + cq
        zero = jnp.zeros_like(qs)
        for v, f in enumerate((fq.astype(bf16), zero, (-fq).astype(bf16))):
            qv_ref[v] = jnp.where(lo, qs, f)
            qv_ref[3 + v] = jnp.where(lo, f, qs)
        d = (lax.broadcasted_iota(jnp.int32, (tq, tq), 0)
             - lax.broadcasted_iota(jnp.int32, (tq, tq), 1)).astype(f32)
        diag_bias = -cvec_ref[2:3, 0:1] * jnp.abs(d)

        if cur_s_ref is not None:
            m_cur = [jnp.max(cur_m_ref[c], axis=1, keepdims=True) for c in range(2)]
            acc = [None, None]
        m_acc = [None, None]
        for r in range(nkt):
            if cur_s_ref is not None:
                v_ones = vaug_ref[key_rows(cur_tile, r), :]
                for c in range(2):
                    e = jnp.exp2(cur_s_ref[c, r] - m_cur[c]).astype(bf16)
                    pv = _dot(e, v_ones)
                    acc[c] = pv if acc[c] is None else acc[c] + pv
            rows = key_rows(nxt_tile, r)
            variant = 1 if r == 0 else jnp.where(nxt_tile + r >= nkt, 0, 2)
            for c in range(2):
                s = _nt_dot(qv_ref[3 * c + variant], kaug_ref[c, rows, :])
                if r == 0:
                    s = s + diag_bias
                nxt_s_ref[c, r] = s
                m_acc[c] = halves_max(s) if r == 0 else jnp.maximum(m_acc[c], halves_max(s))
        for c in range(2):
            nxt_m_ref[c] = m_acc[c]
        if cur_s_ref is None:
            return

        r1 = 1.0 / acc[0][:, DIFF_V_DIM:DIFF_V_DIM + 1]
        r2 = lam_ref[0] / acc[1][:, DIFF_V_DIM:DIFF_V_DIM + 1]
        out = acc[0][:, :DIFF_V_DIM] * r1 - acc[1][:, :DIFF_V_DIM] * r2
        y = out * lax.rsqrt(jnp.mean(out * out, axis=-1, keepdims=True) + NORM_EPS)
        out_ref[...] = (y * nw_ref[...] * out_scale).astype(out_ref.dtype)

    @pl.when(qi == 0)
    def _():
        kpos = lax.broadcasted_iota(jnp.int32, (S, 1), 0)
        fk = (_diff_position_features(kpos, lane_in_half, 2 * DIFF_SLOPE_PARTS) + ck).astype(bf16)
        k = k_ref[...]
        kaug_ref[0] = jnp.where(lo, k, fk)
        kaug_ref[1] = jnp.where(lo, fk, k)
        vaug_ref[:, 0:DIFF_V_DIM] = v_ref[...]
        vaug_ref[:, DIFF_V_DIM:] = jnp.broadcast_to(jnp.where(lane == 0, 1.0, 0.0).astype(bf16), (S, LANES))
        pipeline(q_ref.at[0:tq], 0, s_even_ref, m_even_ref, None, None, None, None)

    even = 2 * qi
    pipeline(q_ref.at[tq:2 * tq], even + 1, s_odd_ref, m_odd_ref, even, s_even_ref, m_even_ref, o_ref.at[0:tq])
    pipeline(qnext_ref, even + 2, s_even_ref, m_even_ref, even + 1, s_odd_ref, m_odd_ref, o_ref.at[tq:2 * tq])


def diff_slope_vectors(slopes):
    c = slopes.astype(f32) * LOG2E
    parts = []
    rest = c
    for _ in range(DIFF_SLOPE_PARTS):
        p = rest.astype(bf16).astype(f32)
        parts.append(p)
        rest = rest - p
    parts = jnp.stack(parts, axis=1)
    lane_in_half = np.arange(LANES) % HEAD_DIM
    n = 2 * DIFF_SLOPE_PARTS
    q_sel = np.where((lane_in_half >= n) & (lane_in_half < 2 * n), (lane_in_half - n) // 2, -1)
    k_sel = np.where(lane_in_half < n, lane_in_half // 2, -1)
    zero = jnp.zeros((slopes.shape[0], 1), f32)
    padded = jnp.concatenate([parts, zero], axis=1)
    cq = padded[:, q_sel]
    ck = -padded[:, k_sel]
    return jnp.stack([cq, ck, jnp.broadcast_to(c[:, None], cq.shape)], axis=1)


def diff_attention(proj, cvec, lam, norm_w, *, B, S, lam_init):
    T = B * S
    tq = DIFF_TQ
    nq = S // tq
    assert S <= DIFF_POS_RADIX * DIFF_POS_RADIX and nq & (nq - 1) == 0 and nq % 2 == 0
    steps = nq // 2
    blk0 = DIFF_COL0 // LANES
    smem = pl.BlockSpec(memory_space=pltpu.SMEM)
    return pl.pallas_call(
        functools.partial(_diff_kernel, S=S, out_scale=1.0 - lam_init),
        grid=(B, DIFF_HEADS, steps),
        in_specs=[smem,
                  pl.BlockSpec((2 * tq, LANES), lambda b, h, i: (b * steps + i, blk0 + h)),
                  pl.BlockSpec((tq, LANES), lambda b, h, i: (b * nq + jnp.minimum(2 * i + 2, nq - 1), blk0 + h)),
                  pl.BlockSpec((S, LANES), lambda b, h, i: (b, blk0 + DIFF_HEADS + h)),
                  pl.BlockSpec((S, LANES), lambda b, h, i: (b, blk0 + 2 * DIFF_HEADS + h)),
                  pl.BlockSpec((None, 3, LANES), lambda b, h, i: (h, 0, 0)),
                  pl.BlockSpec((1, LANES), lambda b, h, i: (0, h))],
        out_specs=pl.BlockSpec((2 * tq, LANES), lambda b, h, i: (b * steps + i, h)),
        out_shape=jax.ShapeDtypeStruct((T, DIFF_W), bf16),
        scratch_shapes=[pltpu.VMEM((2, S, LANES), bf16),
                        pltpu.VMEM((S, 2 * LANES), bf16),
                        pltpu.VMEM((6, tq, LANES), bf16),
                        pltpu.VMEM((2, nq, tq, tq), f32),
                        pltpu.VMEM((2, nq, tq, tq), f32),
                        pltpu.VMEM((2, tq, LANES), f32),
                        pltpu.VMEM((2, tq, LANES), f32)],
        compiler_params=pltpu.CompilerParams(
            dimension_semantics=("parallel", "parallel", "arbitrary"), vmem_limit_bytes=VMEM_LIMIT),
        name="diff_attention",
    )(lam, proj, proj, proj, proj, cvec, norm_w.reshape(1, DIFF_W))


def _out_proj_kernel(x_ref, r_ref, n_ref, d_ref, w_ref, o_ref):
    acc = _dot(r_ref[...], w_ref[0:RET_W, :])
    acc += _dot(n_ref[...], w_ref[RET_W:RET_W + NA_W, :])
    acc += _dot(d_ref[...], w_ref[RET_W + NA_W:, :])
    o_ref[...] = x_ref[...] + acc


def out_proj(x, ret, na, diff, w_out, *, tm=512):
    T, D = x.shape
    row = lambda i: (i, 0)
    return pl.pallas_call(
        _out_proj_kernel,
        grid=(T // tm,),
        in_specs=[pl.BlockSpec((tm, D), row),
                  pl.BlockSpec((tm, RET_W), row),
                  pl.BlockSpec((tm, NA_W), row),
                  pl.BlockSpec((tm, DIFF_W), row),
                  pl.BlockSpec((D, D), lambda i: (0, 0))],
        out_specs=pl.BlockSpec((tm, D), row),
        out_shape=jax.ShapeDtypeStruct((T, D), f32),
        compiler_params=pltpu.CompilerParams(
            dimension_semantics=("parallel",), vmem_limit_bytes=VMEM_LIMIT),
        name="out_proj",
    )(x, ret, na, diff, w_out)


def _rms(x):
    return x * lax.rsqrt(jnp.mean(x * x, axis=-1, keepdims=True) + NORM_EPS)


def _ffn_gate_kernel(x_ref, xp_ref, xn_ref, nw_ref, wa_ref, wg_ref, cw_ref, cb_ref, o_ref, h_ref, hh_ref,
                     *, S, tm):
    i = pl.program_id(0)

    @pl.when(pl.program_id(1) == 0)
    def _():
        nw = nw_ref[...]
        h_ref[...] = (_rms(x_ref[...]) * nw).astype(bf16)
        hh_ref[0:F32_SUBLANES, :] = (_rms(xp_ref[...]) * nw).astype(bf16)
        hh_ref[F32_SUBLANES:, :] = (_rms(xn_ref[...]) * nw).astype(bf16)

    h = h_ref[...]
    hh = hh_ref[...]
    row = lax.broadcasted_iota(jnp.int32, (tm, 1), 0)
    t0 = i * tm
    at_seq_start = (t0 % S) == 0
    at_seq_end = ((t0 + tm) % S) == 0
    for c0 in range(0, o_ref.shape[1], MXU_WIDTH):
        cols = slice(c0, c0 + MXU_WIDTH)
        wg = wg_ref[:, cols]
        g = _dot(h, wg)
        g_halo = _dot(hh, wg)
        a = _dot(h, wa_ref[:, cols])
        prev = jnp.where(at_seq_start, 0.0, g_halo[F32_SUBLANES - 1:F32_SUBLANES, :])
        nxt = jnp.where(at_seq_end, 0.0, g_halo[F32_SUBLANES:F32_SUBLANES + 1, :])
        g_up = jnp.where(row == 0, prev, pltpu.roll(g, 1, 0))
        g_dn = jnp.where(row == tm - 1, nxt, pltpu.roll(g, tm - 1, 0))
        conv = g_up * cw_ref[0:1, cols] + g * cw_ref[1:2, cols] + g_dn * cw_ref[2:3, cols] + cb_ref[:, cols]
        o_ref[:, cols] = (jax.nn.gelu(conv) * a).astype(o_ref.dtype)


def ffn_gate(x, nw, w_up, conv_w, conv_b, *, S, tm=1024, tf=512):
    T, D = x.shape
    nf = D_FF // tf
    rb = tm // F32_SUBLANES
    last_rb = T // F32_SUBLANES - 1
    assert S % tm == 0
    return pl.pallas_call(
        functools.partial(_ffn_gate_kernel, S=S, tm=tm),
        grid=(T // tm, nf),
        in_specs=[pl.BlockSpec((tm, D), lambda i, j: (i, 0)),
                  pl.BlockSpec((F32_SUBLANES, D), lambda i, j: (jnp.maximum(i * rb - 1, 0), 0)),
                  pl.BlockSpec((F32_SUBLANES, D), lambda i, j: (jnp.minimum((i + 1) * rb, last_rb), 0)),
                  pl.BlockSpec((1, D), lambda i, j: (0, 0)),
                  pl.BlockSpec((D, tf), lambda i, j: (0, j)),
                  pl.BlockSpec((D, tf), lambda i, j: (0, nf + j)),
                  pl.BlockSpec((3, tf), lambda i, j: (0, j)),
                  pl.BlockSpec((1, tf), lambda i, j: (0, j))],
        out_specs=pl.BlockSpec((tm, tf), lambda i, j: (i, j)),
        out_shape=jax.ShapeDtypeStruct((T, D_FF), bf16),
        scratch_shapes=[pltpu.VMEM((tm, D), bf16), pltpu.VMEM((2 * F32_SUBLANES, D), bf16)],
        compiler_params=pltpu.CompilerParams(
            dimension_semantics=("parallel", "arbitrary"), vmem_limit_bytes=VMEM_LIMIT),
        name="ffn_gate",
    )(x, x, x, nw.reshape(1, D), w_up, w_up, conv_w, conv_b.reshape(1, D_FF))


def _ffn_down_kernel(x_ref, g_ref, wd_ref, fw_ref, o_ref, *, final_norm):
    k = pl.program_id(1)

    @pl.when(k == 0)
    def _():
        o_ref[...] = x_ref[...]

    o_ref[...] += _dot(g_ref[...], wd_ref[...])

    if final_norm:
        @pl.when(k == pl.num_programs(1) - 1)
        def _():
            x = o_ref[...]
            y = x * lax.rsqrt(jnp.mean(x * x, axis=-1, keepdims=True) + NORM_EPS)
            o_ref[...] = y * fw_ref[...]


def ffn_down(x, gated, w_down, final_w, *, final_norm, tm=512, tk=2816):
    T, D = x.shape
    nk = D_FF // tk
    return pl.pallas_call(
        functools.partial(_ffn_down_kernel, final_norm=final_norm),
        grid=(T // tm, nk),
        in_specs=[pl.BlockSpec((tm, D), lambda i, k: (i, 0)),
                  pl.BlockSpec((tm, tk), lambda i, k: (i, k)),
                  pl.BlockSpec((tk, D), lambda i, k: (k, 0)),
                  pl.BlockSpec((1, D), lambda i, k: (0, 0))],
        out_specs=pl.BlockSpec((tm, D), lambda i, k: (i, 0)),
        out_shape=jax.ShapeDtypeStruct((T, D), f32),
        compiler_params=pltpu.CompilerParams(
            dimension_semantics=("parallel", "arbitrary"), vmem_limit_bytes=VMEM_LIMIT),
        name="ffn_down",
    )(x, gated, w_down, final_w.reshape(1, D))


def _prepare_layer(l, norm1_w, w_in, ret_decay_fwd, ret_decay_bwd, ret_norm_w, na_rpb,
                   lq1, lk1, lq2, lk2, diff_norm_w, w_out, norm2_w, ffn_w_up, ffn_conv_w,
                   ffn_conv_b, ffn_w_down):
    r0, n0, d0 = 0, 4 * RET_W, 4 * RET_W + 3 * NA_W
    w = w_in[l]
    w_perm = jnp.concatenate([w[:, n0:d0], w[:, d0:], w[:, r0:n0]], axis=1).astype(bf16)
    lam_init = 0.8 - 0.6 * math.exp(-0.3 * l)
    lam = (jnp.exp(jnp.sum(lq1[l].astype(f32) * lk1[l].astype(f32)))
           - jnp.exp(jnp.sum(lq2[l].astype(f32) * lk2[l].astype(f32))) + lam_init)
    return dict(
        norm1_w=norm1_w[l].astype(f32), w_in=w_perm,
        lgf=jax.nn.log_sigmoid(ret_decay_fwd[l].astype(f32)),
        lgb=jax.nn.log_sigmoid(ret_decay_bwd[l].astype(f32)),
        ret_norm_w=ret_norm_w[l].astype(f32),
        na_bias=na_bias_table(na_rpb[l]),
        lam=lam.reshape(1), lam_init=lam_init, diff_norm_w=diff_norm_w[l].astype(f32),
        w_out=w_out[l].astype(bf16), norm2_w=norm2_w[l].astype(f32),
        w_up=ffn_w_up[l].astype(bf16), conv_w=ffn_conv_w[l].astype(f32),
        conv_b=ffn_conv_b[l].astype(f32), w_down=ffn_w_down[l].astype(bf16))


def _trunk(x, layers, final_norm_w, slope_vecs):
    B, S, D = x.shape
    x = x.reshape(B * S, D)
    for l, p in enumerate(layers):
        proj = norm_matmul(x, p["norm1_w"], p["w_in"])
        ret = retention(proj, p["lgf"], p["lgb"], p["ret_norm_w"], B=B, S=S)
        na = neighbourhood_attention(proj, p["na_bias"], B=B, S=S)
        diff = diff_attention(proj, slope_vecs, p["lam"], p["diff_norm_w"], B=B, S=S, lam_init=p["lam_init"])
        x = out_proj(x, ret, na, diff, p["w_out"])
        gated = ffn_gate(x, p["norm2_w"], p["w_up"], p["conv_w"], p["conv_b"], S=S)
        x = ffn_down(x, gated, p["w_down"], final_norm_w.astype(f32), final_norm=(l == len(layers) - 1))
    return x.reshape(B, S, D)


def kernel(x_prompt, x_sample, norm1_w, w_in, ret_decay_fwd, ret_decay_bwd, ret_norm_w, na_rpb, diff_lambda_q1, diff_lambda_k1, diff_lambda_q2, diff_lambda_k2, diff_norm_w, w_out, norm2_w, ffn_w_up, ffn_conv_w, ffn_conv_b, ffn_w_down, final_norm_w):
    layers = [_prepare_layer(l, norm1_w, w_in, ret_decay_fwd, ret_decay_bwd, ret_norm_w, na_rpb,
                             diff_lambda_q1, diff_lambda_k1, diff_lambda_q2, diff_lambda_k2,
                             diff_norm_w, w_out, norm2_w, ffn_w_up, ffn_conv_w, ffn_conv_b, ffn_w_down)
              for l in range(w_in.shape[0])]
    slopes = 2.0 ** (-8.0 * (jnp.arange(DIFF_HEADS, dtype=f32) + 1.0) / DIFF_HEADS)
    slope_vecs = diff_slope_vectors(slopes)
    y_prompt = _trunk(x_prompt, layers, final_norm_w, slope_vecs)
    y_sample = _trunk(x_sample, layers, final_norm_w, slope_vecs)
    return (y_prompt, y_sample)
```

```python
import functools
import math

import jax
import jax.numpy as jnp
import numpy as np
from jax import lax
from jax.experimental import pallas as pl
from jax.experimental.pallas import tpu as pltpu

D_MODEL = 2048
DEPTH = 2
HEAD_DIM = 64
RET_HEADS = 8
RET_W = RET_HEADS * HEAD_DIM
NA_HEADS = 12
NA_W = NA_HEADS * HEAD_DIM
GRID_W = 64
NA_KH = 8
NA_KW = 16
DIFF_HEADS = 6
DIFF_V_DIM = 2 * HEAD_DIM
DIFF_W = DIFF_HEADS * DIFF_V_DIM
D_IN = 4 * RET_W + 3 * NA_W + 3 * DIFF_W
D_FF = 5632
NORM_EPS = 1e-6
NEG_INF = -1e30

LANES = 128
F32_SUBLANES = 8
MXU_WIDTH = 256
RET_CHUNK = 256
RET_UNROLL = 4
NA_ROWS_PER_STEP = 4
DIFF_TQ = 256
DIFF_POS_RADIX = 64
DIFF_SLOPE_PARTS = 3
LOG2E = 1.4426950408889634
VMEM_LIMIT = 56 * 1024 * 1024

NA_COL0 = 0
DIFF_COL0 = 3 * NA_W
RET_COL0 = 3 * NA_W + 3 * DIFF_W

f32 = jnp.float32
bf16 = jnp.bfloat16


def _nt_dot(a, b):
    return lax.dot_general(a, b, (((1,), (1,)), ((), ())), preferred_element_type=f32)


def _tn_dot(a, b):
    return lax.dot_general(a, b, (((0,), (0,)), ((), ())), preferred_element_type=f32)


def _dot(a, b):
    return jnp.dot(a, b, preferred_element_type=f32)


def _norm_matmul_kernel(x_ref, nw_ref, w_ref, o_ref, h_ref):
    @pl.when(pl.program_id(1) == 0)
    def _():
        x = x_ref[...]
        y = x * lax.rsqrt(jnp.mean(x * x, axis=-1, keepdims=True) + NORM_EPS)
        h_ref[...] = (y * nw_ref[...]).astype(bf16)

    o_ref[...] = _dot(h_ref[...], w_ref[...]).astype(o_ref.dtype)


def norm_matmul(x, nw, w, layer, *, tm=512, tn=3328):
    T, D = x.shape
    N = w.shape[2]
    return pl.pallas_call(
        _norm_matmul_kernel,
        grid=(T // tm, N // tn),
        in_specs=[
            pl.BlockSpec((tm, D), lambda i, j: (i, 0)),
            pl.BlockSpec((1, D), lambda i, j: (0, 0)),
            pl.BlockSpec((None, D, tn), lambda i, j: (layer, 0, j)),
        ],
        out_specs=pl.BlockSpec((tm, tn), lambda i, j: (i, j)),
        out_shape=jax.ShapeDtypeStruct((T, N), bf16),
        scratch_shapes=[pltpu.VMEM((tm, D), bf16)],
        compiler_params=pltpu.CompilerParams(
            dimension_semantics=("parallel", "arbitrary"), vmem_limit_bytes=VMEM_LIMIT),
        name="norm_matmul",
    )(x, nw.reshape(1, D), w)


def _retention_kernel(lgf_ref, lgb_ref, q_ref, k_ref, v_ref, g_ref, nw_ref, o_ref, acc_ref, *, S):
    C = RET_CHUNK
    n = S // C
    pair = pl.program_id(1)
    scale = HEAD_DIM ** -0.5

    lane = lax.broadcasted_iota(jnp.int32, (1, LANES), 1)
    lo = lane < HEAD_DIM
    lgf = jnp.where(lo, lgf_ref[2 * pair], lgf_ref[2 * pair + 1])
    lgb = jnp.where(lo, lgb_ref[2 * pair], lgb_ref[2 * pair + 1])

    pos = lax.broadcasted_iota(jnp.int32, (C, 1), 0).astype(f32)
    qdec_f = jnp.exp(lgf * (pos + 1.0))
    kdec_f = jnp.exp(lgf * (C - 1.0 - pos)) * scale
    cdec_f = jnp.exp(lgf * float(C))
    qdec_b = jnp.exp(lgb * (C - pos))
    kdec_b = jnp.exp(lgb * pos) * scale
    cdec_b = jnp.exp(lgb * float(C))

    ri = lax.broadcasted_iota(jnp.int32, (C, C), 0)
    ci = lax.broadcasted_iota(jnp.int32, (C, C), 1)
    delta = (ri - ci).astype(f32)

    def decay_mat(h):
        fwd = jnp.exp(lgf_ref[2 * pair + h] * jnp.maximum(delta, 0.0))
        bwd = jnp.exp(lgb_ref[2 * pair + h] * jnp.maximum(-delta, 0.0))
        return jnp.where(delta >= 0, fwd, bwd) * scale

    dmat0 = decay_mat(0)
    dmat1 = decay_mat(1)

    r2 = lax.broadcasted_iota(jnp.int32, (LANES, LANES), 0) // HEAD_DIM
    c2 = lax.broadcasted_iota(jnp.int32, (LANES, LANES), 1) // HEAD_DIM
    same_head = r2 == c2
    avg = jnp.where(same_head, 1.0 / HEAD_DIM, 0.0).astype(bf16)

    def chunk(ref, c):
        return ref[pl.ds(pl.multiple_of(c * C, C), C), :]

    def fwd_body(c, state):
        qc, kc, vc = chunk(q_ref, c), chunk(k_ref, c), chunk(v_ref, c)
        zero = jnp.zeros_like(qc)
        in0 = _nt_dot(jnp.where(lo, qc, zero), kc)
        in1 = _nt_dot(jnp.where(lo, zero, qc), kc)
        w = jnp.concatenate([(in0 * dmat0).astype(bf16), (in1 * dmat1).astype(bf16)], axis=1)
        vv = jnp.concatenate([jnp.where(lo, vc, zero), jnp.where(lo, zero, vc)], axis=0)
        intra = _dot(w, vv)
        qd = (qc.astype(f32) * qdec_f).astype(bf16)
        cross = _dot(qd, state.astype(bf16))
        acc_ref[pl.ds(pl.multiple_of(c * C, C), C), :] = intra + cross
        kd = (kc.astype(f32) * kdec_f).astype(bf16)
        upd = _tn_dot(kd, vc)
        return state * cdec_f + jnp.where(same_head, upd, 0.0)

    lax.fori_loop(0, n, fwd_body, jnp.zeros((LANES, LANES), f32), unroll=RET_UNROLL)

    def lane_mean(y):
        hi = y.astype(bf16)
        lo_part = (y - hi.astype(f32)).astype(bf16)
        return _dot(hi, avg) + _dot(lo_part, avg)

    def bwd_body(t, state):
        c = n - 1 - t
        qc, kc, vc = chunk(q_ref, c), chunk(k_ref, c), chunk(v_ref, c)
        qd = (qc.astype(f32) * qdec_b).astype(bf16)
        y = chunk(acc_ref, c) + _dot(qd, state.astype(bf16))
        y = y - lane_mean(y)
        y = y * lax.rsqrt(lane_mean(y * y) + NORM_EPS)
        gc = chunk(g_ref, c).astype(f32)
        out = jax.nn.silu(gc) * (y * nw_ref[...])
        o_ref[pl.ds(pl.multiple_of(c * C, C), C), :] = out.astype(o_ref.dtype)
        kd = (kc.astype(f32) * kdec_b).astype(bf16)
        upd = _tn_dot(kd, vc)
        return state * cdec_b + jnp.where(same_head, upd, 0.0)

    lax.fori_loop(0, n, bwd_body, jnp.zeros((LANES, LANES), f32), unroll=RET_UNROLL)


def retention(proj, lgf, lgb, norm_w, *, B, S):
    T = B * S
    blk0 = RET_COL0 // LANES
    npair = RET_HEADS // 2

    def col(seg):
        return lambda b, p: (b, blk0 + seg * npair + p)

    smem = pl.BlockSpec(memory_space=pltpu.SMEM)
    return pl.pallas_call(
        functools.partial(_retention_kernel, S=S),
        grid=(B, npair),
        in_specs=[smem, smem,
                  pl.BlockSpec((S, LANES), col(0)),
                  pl.BlockSpec((S, LANES), col(1)),
                  pl.BlockSpec((S, LANES), col(2)),
                  pl.BlockSpec((S, LANES), col(3)),
                  pl.BlockSpec((1, LANES), lambda b, p: (0, p))],
        out_specs=pl.BlockSpec((S, LANES), lambda b, p: (b, p)),
        out_shape=jax.ShapeDtypeStruct((T, RET_W), bf16),
        scratch_shapes=[pltpu.VMEM((S, LANES), f32)],
        compiler_params=pltpu.CompilerParams(
            dimension_semantics=("parallel", "parallel"), vmem_limit_bytes=VMEM_LIMIT),
        name="retention",
    )(lgf, lgb, proj, proj, proj, proj, norm_w.reshape(1, RET_W))


def _na_window_start(r, rows):
    return jnp.clip(r - NA_KH // 2, 0, rows - NA_KH)


def _na_kernel(q_ref, k_ref, v_ref, bias_ref, o_ref, *, rows):
    g = pl.program_id(1)
    nwin = NA_KH * GRID_W
    lane = lax.broadcasted_iota(jnp.int32, (1, LANES), 1)
    lo = lane < HEAD_DIM
    scale = HEAD_DIM ** -0.5
    npair = NA_HEADS // 2
    for j in range(NA_ROWS_PER_STEP):
        r = g * NA_ROWS_PER_STEP + j
        start = pl.multiple_of(_na_window_start(r, rows) * GRID_W, GRID_W)
        tok = slice(j * GRID_W, (j + 1) * GRID_W)
        parts = []
        for p in range(npair):
            cols = slice(p * LANES, (p + 1) * LANES)
            qp = (q_ref[tok, cols].astype(f32) * scale).astype(bf16)
            kp = k_ref[pl.ds(start, nwin), cols]
            zero = jnp.zeros_like(qp)
            parts.append(_nt_dot(jnp.where(lo, qp, zero), kp))
            parts.append(_nt_dot(jnp.where(lo, zero, qp), kp))
        d0 = NA_KH - 1 - (r - _na_window_start(r, rows))
        bias = jnp.concatenate(
            [jnp.concatenate([bias_ref[h, d0 + w] for w in range(0, NA_KH, 2)], axis=1)
             for h in range(NA_HEADS)], axis=0)
        s = jnp.concatenate(parts, axis=0) + bias
        e = jnp.exp(s - jnp.max(s, axis=-1, keepdims=True))
        inv = 1.0 / jnp.sum(e, axis=-1, keepdims=True)
        eb = e.astype(bf16)
        for p in range(npair):
            cols = slice(p * LANES, (p + 1) * LANES)
            vp = v_ref[pl.ds(start, nwin), cols]
            r0, r1, r2 = 2 * p * GRID_W, (2 * p + 1) * GRID_W, (2 * p + 2) * GRID_W
            even = _dot(eb[r0:r1], vp) * inv[r0:r1]
            odd = _dot(eb[r1:r2], vp) * inv[r1:r2]
            o_ref[tok, cols] = jnp.where(lo, even, odd).astype(o_ref.dtype)


def neighbourhood_attention(proj, bias_table, *, B, S):
    T = B * S
    rows = S // GRID_W
    nwin = NA_KH * GRID_W
    R = NA_ROWS_PER_STEP
    steps = rows // R
    assert rows % R == 0 and rows >= NA_KH

    return pl.pallas_call(
        functools.partial(_na_kernel, rows=rows),
        grid=(B, steps),
        in_specs=[pl.BlockSpec((R * GRID_W, NA_W), lambda b, g: (b * steps + g, 0)),
                  pl.BlockSpec((S, NA_W), lambda b, g: (b, 1)),
                  pl.BlockSpec((S, NA_W), lambda b, g: (b, 2)),
                  pl.BlockSpec(bias_table.shape, lambda b, g: (0, 0, 0, 0))],
        out_specs=pl.BlockSpec((R * GRID_W, NA_W), lambda b, g: (b * steps + g, 0)),
        out_shape=jax.ShapeDtypeStruct((T, NA_W), bf16),
        compiler_params=pltpu.CompilerParams(
            dimension_semantics=("parallel", "arbitrary"), vmem_limit_bytes=VMEM_LIMIT),
        name="neighbourhood_attention",
    )(proj, proj, proj, bias_table)


def na_bias_table(rpb):
    qcol = np.arange(GRID_W)[:, None]
    kcol = np.arange(GRID_W)[None, :]
    cs = np.clip(qcol - NA_KW // 2, 0, GRID_W - NA_KW)
    in_win = (kcol >= cs) & (kcol < cs + NA_KW)
    dc = np.clip(kcol - qcol + NA_KW - 1, 0, 2 * NA_KW - 2)
    blocks = jnp.where(jnp.asarray(in_win), rpb.astype(f32)[:, :, dc], NEG_INF)
    return jnp.concatenate([blocks[:, :-1], blocks[:, 1:]], axis=-1)


def _diff_position_features(pos, lane_in_half, first):
    f = lane_in_half - first
    lo = jnp.bitwise_and(pos, DIFF_POS_RADIX - 1)
    hi = pos - lo
    val = jnp.where(jnp.bitwise_and(f, 1) == 0, hi, lo)
    return jnp.where((f >= 0) & (f < 2 * DIFF_SLOPE_PARTS), val, 0).astype(f32)


def _diff_kernel(lam_ref, q_ref, qnext_ref, k_ref, v_ref, cvec_ref, nw_ref, o_ref,
                 kaug_ref, vaug_ref, qv_ref, s_even_ref, s_odd_ref, m_even_ref, m_odd_ref, *, S, out_scale):
    tq = DIFF_TQ
    nkt = S // tq
    qi = pl.program_id(2)
    lane = lax.broadcasted_iota(jnp.int32, (1, LANES), 1)
    lo = lane < HEAD_DIM
    lane_in_half = jnp.bitwise_and(lane, HEAD_DIM - 1)
    cq = cvec_ref[0:1, :]
    ck = cvec_ref[1:2, :]

    def key_rows(tile, r):
        t = jnp.bitwise_and(tile + r, nkt - 1)
        return pl.ds(pl.multiple_of(t * tq, tq), tq)

    def halves_max(x):
        return jnp.maximum(x[:, :LANES], x[:, LANES:])

    def pipeline(q_blk_ref, nxt_tile, nxt_s_ref, nxt_m_ref, cur_tile, cur_s_ref, cur_m_ref, out_ref):
        if q_blk_ref is not None:
            qs = (q_blk_ref[...].astype(f32) * (HEAD_DIM ** -0.5 * LOG2E)).astype(bf16)
            qpos = nxt_tile * tq + lax.broadcasted_iota(jnp.int32, (tq, 1), 0)
            fq = _diff_position_features(qpos, lane_in_half, 0) + cq
            zero = jnp.zeros_like(qs)
            for v, f in enumerate((fq.astype(bf16), zero, (-fq).astype(bf16))):
                qv_ref[v] = jnp.where(lo, qs, f)
                qv_ref[3 + v] = jnp.where(lo, f, qs)
            d = (lax.broadcasted_iota(jnp.int32, (tq, tq), 0)
                 - lax.broadcasted_iota(jnp.int32, (tq, tq), 1)).astype(f32)
            diag_bias = -cvec_ref[2:3, 0:1] * jnp.abs(d)

        if cur_s_ref is not None:
            m_cur = [jnp.max(cur_m_ref[c], axis=1, keepdims=True) for c in range(2)]
            acc = [None, None]
        m_acc = [None, None]
        for r in range(nkt):
            if cur_s_ref is not None:
                v_ones = vaug_ref[key_rows(cur_tile, r), :]
                for c in range(2):
                    e = jnp.exp2(cur_s_ref[c, r] - m_cur[c]).astype(bf16)
                    pv = _dot(e, v_ones)
                    acc[c] = pv if acc[c] is None else acc[c] + pv
            if q_blk_ref is None:
                continue
            rows = key_rows(nxt_tile, r)
            variant = 1 if r == 0 else jnp.where(nxt_tile + r >= nkt, 0, 2)
            for c in range(2):
                s = _nt_dot(qv_ref[3 * c + variant], kaug_ref[c, rows, :])
                if r == 0:
                    s = s + diag_bias
                nxt_s_ref[c, r] = s
                m_acc[c] = halves_max(s) if r == 0 else jnp.maximum(m_acc[c], halves_max(s))
        if q_blk_ref is not None:
            for c in range(2):
                nxt_m_ref[c] = m_acc[c]
        if cur_s_ref is None:
            return

        r1 = 1.0 / acc[0][:, DIFF_V_DIM:DIFF_V_DIM + 1]
        r2 = lam_ref[0] / acc[1][:, DIFF_V_DIM:DIFF_V_DIM + 1]
        out = acc[0][:, :DIFF_V_DIM] * r1 - acc[1][:, :DIFF_V_DIM] * r2
        y = out * lax.rsqrt(jnp.mean(out * out, axis=-1, keepdims=True) + NORM_EPS)
        out_ref[...] = (y * nw_ref[...] * out_scale).astype(out_ref.dtype)

    @pl.when(qi == 0)
    def _():
        kpos = lax.broadcasted_iota(jnp.int32, (S, 1), 0)
        fk = (_diff_position_features(kpos, lane_in_half, 2 * DIFF_SLOPE_PARTS) + ck).astype(bf16)
        k = k_ref[...]
        kaug_ref[0] = jnp.where(lo, k, fk)
        kaug_ref[1] = jnp.where(lo, fk, k)
        vaug_ref[:, 0:DIFF_V_DIM] = v_ref[...]
        vaug_ref[:, DIFF_V_DIM:] = jnp.broadcast_to(jnp.where(lane == 0, 1.0, 0.0).astype(bf16), (S, LANES))
        pipeline(q_ref.at[0:tq], 0, s_even_ref, m_even_ref, None, None, None, None)

    even = 2 * qi
    last = nkt // 2 - 1
    pipeline(q_ref.at[tq:2 * tq], even + 1, s_odd_ref, m_odd_ref, even, s_even_ref, m_even_ref, o_ref.at[0:tq])

    @pl.when(qi < last)
    def _():
        pipeline(qnext_ref, even + 2, s_even_ref, m_even_ref, even + 1, s_odd_ref, m_odd_ref,
                 o_ref.at[tq:2 * tq])

    @pl.when(qi == last)
    def _():
        pipeline(None, None, None, None, even + 1, s_odd_ref, m_odd_ref, o_ref.at[tq:2 * tq])


def diff_slope_vectors(slopes):
    c = slopes.astype(f32) * LOG2E
    parts = []
    rest = c
    for _ in range(DIFF_SLOPE_PARTS):
        p = rest.astype(bf16).astype(f32)
        parts.append(p)
        rest = rest - p
    parts = jnp.stack(parts, axis=1)
    lane_in_half = np.arange(LANES) % HEAD_DIM
    n = 2 * DIFF_SLOPE_PARTS
    q_sel = np.where((lane_in_half >= n) & (lane_in_half < 2 * n), (lane_in_half - n) // 2, -1)
    k_sel = np.where(lane_in_half < n, lane_in_half // 2, -1)
    zero = jnp.zeros((slopes.shape[0], 1), f32)
    padded = jnp.concatenate([parts, zero], axis=1)
    cq = padded[:, q_sel]
    ck = -padded[:, k_sel]
    return jnp.stack([cq, ck, jnp.broadcast_to(c[:, None], cq.shape)], axis=1)


def diff_attention(proj, cvec, lam, norm_w, *, B, S, lam_init):
    T = B * S
    tq = DIFF_TQ
    nq = S // tq
    assert S <= DIFF_POS_RADIX * DIFF_POS_RADIX and nq & (nq - 1) == 0 and nq % 2 == 0
    steps = nq // 2
    blk0 = DIFF_COL0 // LANES
    smem = pl.BlockSpec(memory_space=pltpu.SMEM)
    return pl.pallas_call(
        functools.partial(_diff_kernel, S=S, out_scale=1.0 - lam_init),
        grid=(B, DIFF_HEADS, steps),
        in_specs=[smem,
                  pl.BlockSpec((2 * tq, LANES), lambda b, h, i: (b * steps + i, blk0 + h)),
                  pl.BlockSpec((tq, LANES), lambda b, h, i: (b * nq + jnp.minimum(2 * i + 2, nq - 1), blk0 + h)),
                  pl.BlockSpec((S, LANES), lambda b, h, i: (b, blk0 + DIFF_HEADS + h)),
                  pl.BlockSpec((S, LANES), lambda b, h, i: (b, blk0 + 2 * DIFF_HEADS + h)),
                  pl.BlockSpec((None, 3, LANES), lambda b, h, i: (h, 0, 0)),
                  pl.BlockSpec((1, LANES), lambda b, h, i: (0, h))],
        out_specs=pl.BlockSpec((2 * tq, LANES), lambda b, h, i: (b * steps + i, h)),
        out_shape=jax.ShapeDtypeStruct((T, DIFF_W), bf16),
        scratch_shapes=[pltpu.VMEM((2, S, LANES), bf16),
                        pltpu.VMEM((S, 2 * LANES), bf16),
                        pltpu.VMEM((6, tq, LANES), bf16),
                        pltpu.VMEM((2, nq, tq, tq), f32),
                        pltpu.VMEM((2, nq, tq, tq), f32),
                        pltpu.VMEM((2, tq, LANES), f32),
                        pltpu.VMEM((2, tq, LANES), f32)],
        compiler_params=pltpu.CompilerParams(
            dimension_semantics=("parallel", "parallel", "arbitrary"), vmem_limit_bytes=VMEM_LIMIT),
        name="diff_attention",
    )(lam, proj, proj, proj, proj, cvec, norm_w.reshape(1, DIFF_W))


def _out_proj_kernel(x_ref, r_ref, n_ref, d_ref, w_ref, o_ref):
    acc = _dot(r_ref[...], w_ref[0:RET_W, :])
    acc += _dot(n_ref[...], w_ref[RET_W:RET_W + NA_W, :])
    acc += _dot(d_ref[...], w_ref[RET_W + NA_W:, :])
    o_ref[...] = x_ref[...] + acc


def out_proj(x, ret, na, diff, w_out, layer, *, tm=512):
    T, D = x.shape
    row = lambda i: (i, 0)
    return pl.pallas_call(
        _out_proj_kernel,
        grid=(T // tm,),
        in_specs=[pl.BlockSpec((tm, D), row),
                  pl.BlockSpec((tm, RET_W), row),
                  pl.BlockSpec((tm, NA_W), row),
                  pl.BlockSpec((tm, DIFF_W), row),
                  pl.BlockSpec((None, D, D), lambda i: (layer, 0, 0))],
        out_specs=pl.BlockSpec((tm, D), row),
        out_shape=jax.ShapeDtypeStruct((T, D), f32),
        compiler_params=pltpu.CompilerParams(
            dimension_semantics=("parallel",), vmem_limit_bytes=VMEM_LIMIT),
        name="out_proj",
    )(x, ret, na, diff, w_out)


def _rms(x):
    return x * lax.rsqrt(jnp.mean(x * x, axis=-1, keepdims=True) + NORM_EPS)


def _ffn_gate_kernel(x_ref, xp_ref, xn_ref, nw_ref, wa_ref, wg_ref, cw_ref, cb_ref, o_ref, h_ref, hh_ref,
                     *, S, tm):
    i = pl.program_id(0)

    @pl.when(pl.program_id(1) == 0)
    def _():
        nw = nw_ref[...]
        h_ref[...] = (_rms(x_ref[...]) * nw).astype(bf16)
        hh_ref[0:F32_SUBLANES, :] = (_rms(xp_ref[...]) * nw).astype(bf16)
        hh_ref[F32_SUBLANES:, :] = (_rms(xn_ref[...]) * nw).astype(bf16)

    h = h_ref[...]
    hh = hh_ref[...]
    row = lax.broadcasted_iota(jnp.int32, (tm, 1), 0)
    t0 = i * tm
    at_seq_start = (t0 % S) == 0
    at_seq_end = ((t0 + tm) % S) == 0
    for c0 in range(0, o_ref.shape[1], MXU_WIDTH):
        cols = slice(c0, c0 + MXU_WIDTH)
        wg = wg_ref[:, cols]
        g = _dot(h, wg)
        g_halo = _dot(hh, wg)
        a = _dot(h, wa_ref[:, cols])
        prev = jnp.where(at_seq_start, 0.0, g_halo[F32_SUBLANES - 1:F32_SUBLANES, :])
        nxt = jnp.where(at_seq_end, 0.0, g_halo[F32_SUBLANES:F32_SUBLANES + 1, :])
        g_up = jnp.where(row == 0, prev, pltpu.roll(g, 1, 0))
        g_dn = jnp.where(row == tm - 1, nxt, pltpu.roll(g, tm - 1, 0))
        conv = g_up * cw_ref[0:1, cols] + g * cw_ref[1:2, cols] + g_dn * cw_ref[2:3, cols] + cb_ref[:, cols]
        o_ref[:, cols] = (jax.nn.gelu(conv) * a).astype(o_ref.dtype)


def ffn_gate(x, nw, w_up, layer, conv_w, conv_b, *, S, tm=1024, tf=512):
    T, D = x.shape
    nf = D_FF // tf
    rb = tm // F32_SUBLANES
    last_rb = T // F32_SUBLANES - 1
    assert S % tm == 0
    return pl.pallas_call(
        functools.partial(_ffn_gate_kernel, S=S, tm=tm),
        grid=(T // tm, nf),
        in_specs=[pl.BlockSpec((tm, D), lambda i, j: (i, 0)),
                  pl.BlockSpec((F32_SUBLANES, D), lambda i, j: (jnp.maximum(i * rb - 1, 0), 0)),
                  pl.BlockSpec((F32_SUBLANES, D), lambda i, j: (jnp.minimum((i + 1) * rb, last_rb), 0)),
                  pl.BlockSpec((1, D), lambda i, j: (0, 0)),
                  pl.BlockSpec((None, D, tf), lambda i, j: (layer, 0, j)),
                  pl.BlockSpec((None, D, tf), lambda i, j: (layer, 0, nf + j)),
                  pl.BlockSpec((3, tf), lambda i, j: (0, j)),
                  pl.BlockSpec((1, tf), lambda i, j: (0, j))],
        out_specs=pl.BlockSpec((tm, tf), lambda i, j: (i, j)),
        out_shape=jax.ShapeDtypeStruct((T, D_FF), bf16),
        scratch_shapes=[pltpu.VMEM((tm, D), bf16), pltpu.VMEM((2 * F32_SUBLANES, D), bf16)],
        compiler_params=pltpu.CompilerParams(
            dimension_semantics=("parallel", "arbitrary"), vmem_limit_bytes=VMEM_LIMIT),
        name="ffn_gate",
    )(x, x, x, nw.reshape(1, D), w_up, w_up, conv_w, conv_b.reshape(1, D_FF))


def _ffn_down_kernel(x_ref, g_ref, wd_ref, fw_ref, o_ref, *, final_norm):
    k = pl.program_id(1)

    @pl.when(k == 0)
    def _():
        o_ref[...] = x_ref[...]

    o_ref[...] += _dot(g_ref[...], wd_ref[...])

    if final_norm:
        @pl.when(k == pl.num_programs(1) - 1)
        def _():
            x = o_ref[...]
            y = x * lax.rsqrt(jnp.mean(x * x, axis=-1, keepdims=True) + NORM_EPS)
            o_ref[...] = y * fw_ref[...]


def ffn_down(x, gated, w_down, layer, final_w, *, final_norm, tm=512, tk=2816):
    T, D = x.shape
    nk = D_FF // tk
    return pl.pallas_call(
        functools.partial(_ffn_down_kernel, final_norm=final_norm),
        grid=(T // tm, nk),
        in_specs=[pl.BlockSpec((tm, D), lambda i, k: (i, 0)),
                  pl.BlockSpec((tm, tk), lambda i, k: (i, k)),
                  pl.BlockSpec((None, tk, D), lambda i, k: (layer, k, 0)),
                  pl.BlockSpec((1, D), lambda i, k: (0, 0))],
        out_specs=pl.BlockSpec((tm, D), lambda i, k: (i, 0)),
        out_shape=jax.ShapeDtypeStruct((T, D), f32),
        compiler_params=pltpu.CompilerParams(
            dimension_semantics=("parallel", "arbitrary"), vmem_limit_bytes=VMEM_LIMIT),
        name="ffn_down",
    )(x, gated, w_down, final_w.reshape(1, D))


def _prepare_layer(l, norm1_w, ret_decay_fwd, ret_decay_bwd, ret_norm_w, na_rpb,
                   lq1, lk1, lq2, lk2, diff_norm_w, norm2_w, ffn_conv_w, ffn_conv_b):
    lam_init = 0.8 - 0.6 * math.exp(-0.3 * l)
    lam = (jnp.exp(jnp.sum(lq1[l].astype(f32) * lk1[l].astype(f32)))
           - jnp.exp(jnp.sum(lq2[l].astype(f32) * lk2[l].astype(f32))) + lam_init)
    return dict(
        norm1_w=norm1_w[l].astype(f32),
        lgf=jax.nn.log_sigmoid(ret_decay_fwd[l].astype(f32)),
        lgb=jax.nn.log_sigmoid(ret_decay_bwd[l].astype(f32)),
        ret_norm_w=ret_norm_w[l].astype(f32),
        na_bias=na_bias_table(na_rpb[l]),
        lam=lam.reshape(1), lam_init=lam_init, diff_norm_w=diff_norm_w[l].astype(f32),
        norm2_w=norm2_w[l].astype(f32), conv_w=ffn_conv_w[l].astype(f32), conv_b=ffn_conv_b[l].astype(f32))


def _prepare_weights(w_in, w_out, ffn_w_up, ffn_w_down):
    n0, d0 = 4 * RET_W, 4 * RET_W + 3 * NA_W
    w_in_perm = jnp.concatenate([w_in[:, :, n0:d0], w_in[:, :, d0:], w_in[:, :, :n0]], axis=2)
    return dict(w_in=w_in_perm.astype(bf16), w_out=w_out.astype(bf16),
                w_up=ffn_w_up.astype(bf16), w_down=ffn_w_down.astype(bf16))


def _trunk(x, layers, weights, final_norm_w, slope_vecs):
    B, S, D = x.shape
    x = x.reshape(B * S, D)
    for l, p in enumerate(layers):
        proj = norm_matmul(x, p["norm1_w"], weights["w_in"], l)
        ret = retention(proj, p["lgf"], p["lgb"], p["ret_norm_w"], B=B, S=S)
        na = neighbourhood_attention(proj, p["na_bias"], B=B, S=S)
        diff = diff_attention(proj, slope_vecs, p["lam"], p["diff_norm_w"], B=B, S=S, lam_init=p["lam_init"])
        x = out_proj(x, ret, na, diff, weights["w_out"], l)
        gated = ffn_gate(x, p["norm2_w"], weights["w_up"], l, p["conv_w"], p["conv_b"], S=S)
        x = ffn_down(x, gated, weights["w_down"], l, final_norm_w.astype(f32),
                     final_norm=(l == len(layers) - 1))
    return x.reshape(B, S, D)


def kernel(x_prompt, x_sample, norm1_w, w_in, ret_decay_fwd, ret_decay_bwd, ret_norm_w, na_rpb, diff_lambda_q1, diff_lambda_k1, diff_lambda_q2, diff_lambda_k2, diff_norm_w, w_out, norm2_w, ffn_w_up, ffn_conv_w, ffn_conv_b, ffn_w_down, final_norm_w):
    layers = [_prepare_layer(l, norm1_w, ret_decay_fwd, ret_decay_bwd, ret_norm_w, na_rpb,
                             diff_lambda_q1, diff_lambda_k1, diff_lambda_q2, diff_lambda_k2,
                             diff_norm_w, norm2_w, ffn_conv_w, ffn_conv_b)
              for l in range(w_in.shape[0])]
    weights = _prepare_weights(w_in, w_out, ffn_w_up, ffn_w_down)
    slopes = 2.0 ** (-8.0 * (jnp.arange(DIFF_HEADS, dtype=f32) + 1.0) / DIFF_HEADS)
    slope_vecs = diff_slope_vectors(slopes)
    y_prompt = _trunk(x_prompt, layers, weights, final_norm_w, slope_vecs)
    y_sample = _trunk(x_sample, layers, weights, final_norm_w, slope_vecs)
    return (y_prompt, y_sample)
```

```python
import functools
import math

import jax
import jax.numpy as jnp
import numpy as np
from jax import lax
from jax.experimental import pallas as pl
from jax.experimental.pallas import tpu as pltpu

D_MODEL = 2048
DEPTH = 2
HEAD_DIM = 64
RET_HEADS = 8
RET_W = RET_HEADS * HEAD_DIM
NA_HEADS = 12
NA_W = NA_HEADS * HEAD_DIM
GRID_W = 64
NA_KH = 8
NA_KW = 16
DIFF_HEADS = 6
DIFF_V_DIM = 2 * HEAD_DIM
DIFF_W = DIFF_HEADS * DIFF_V_DIM
D_IN = 4 * RET_W + 3 * NA_W + 3 * DIFF_W
D_FF = 5632
NORM_EPS = 1e-6
NEG_INF = -1e30

LANES = 128
F32_SUBLANES = 8
MXU_WIDTH = 256
RET_CHUNK = 256
RET_UNROLL = 8
NA_ROWS_PER_STEP = 4
DIFF_TQ = 256
DIFF_POS_RADIX = 64
DIFF_SLOPE_PARTS = 3
LOG2E = 1.4426950408889634
GELU_C1 = 2.0 * math.sqrt(2.0 / math.pi)
GELU_C3 = 8.0 * math.sqrt(2.0 / math.pi) * 0.044715
VMEM_LIMIT = 56 * 1024 * 1024

NA_COL0 = 0
DIFF_COL0 = 3 * NA_W
RET_COL0 = 3 * NA_W + 3 * DIFF_W

f32 = jnp.float32
bf16 = jnp.bfloat16


def _nt_dot(a, b):
    return lax.dot_general(a, b, (((1,), (1,)), ((), ())), preferred_element_type=f32)


def _tn_dot(a, b):
    return lax.dot_general(a, b, (((0,), (0,)), ((), ())), preferred_element_type=f32)


def _dot(a, b):
    return jnp.dot(a, b, preferred_element_type=f32)


def _norm_matmul_kernel(x_ref, nw_ref, w_ref, o_ref, h_ref):
    @pl.when(pl.program_id(1) == 0)
    def _():
        x = x_ref[...]
        y = x * lax.rsqrt(jnp.mean(x * x, axis=-1, keepdims=True) + NORM_EPS)
        h_ref[...] = (y * nw_ref[...]).astype(bf16)

    o_ref[...] = _dot(h_ref[...], w_ref[...]).astype(o_ref.dtype)


def norm_matmul(x, nw, w, layer, *, tm=512, tn=3328):
    T, D = x.shape
    N = w.shape[2]
    return pl.pallas_call(
        _norm_matmul_kernel,
        grid=(T // tm, N // tn),
        in_specs=[
            pl.BlockSpec((tm, D), lambda i, j: (i, 0)),
            pl.BlockSpec((1, D), lambda i, j: (0, 0)),
            pl.BlockSpec((None, D, tn), lambda i, j: (layer, 0, j)),
        ],
        out_specs=pl.BlockSpec((tm, tn), lambda i, j: (i, j)),
        out_shape=jax.ShapeDtypeStruct((T, N), bf16),
        scratch_shapes=[pltpu.VMEM((tm, D), bf16)],
        compiler_params=pltpu.CompilerParams(
            dimension_semantics=("parallel", "arbitrary"), vmem_limit_bytes=VMEM_LIMIT),
        name="norm_matmul",
    )(x, nw.reshape(1, D), w)


def _retention_kernel(lgf_ref, lgb_ref, q_ref, k_ref, v_ref, g_ref, nw_ref, o_ref, acc_ref, *, S):
    C = RET_CHUNK
    n = S // C
    pair = pl.program_id(1)
    scale = HEAD_DIM ** -0.5

    lane = lax.broadcasted_iota(jnp.int32, (1, LANES), 1)
    lo = lane < HEAD_DIM
    lgf = jnp.where(lo, lgf_ref[2 * pair], lgf_ref[2 * pair + 1])
    lgb = jnp.where(lo, lgb_ref[2 * pair], lgb_ref[2 * pair + 1])

    pos = lax.broadcasted_iota(jnp.int32, (C, 1), 0).astype(f32)
    qdec_f = jnp.exp(lgf * (pos + 1.0))
    kdec_f = jnp.exp(lgf * (C - 1.0 - pos)) * scale
    cdec_f = jnp.exp(lgf * float(C))
    qdec_b = jnp.exp(lgb * (C - pos))
    kdec_b = jnp.exp(lgb * pos) * scale
    cdec_b = jnp.exp(lgb * float(C))

    ri = lax.broadcasted_iota(jnp.int32, (C, C), 0)
    ci = lax.broadcasted_iota(jnp.int32, (C, C), 1)
    delta = (ri - ci).astype(f32)

    def decay_mat(h):
        fwd = jnp.exp(lgf_ref[2 * pair + h] * jnp.maximum(delta, 0.0))
        bwd = jnp.exp(lgb_ref[2 * pair + h] * jnp.maximum(-delta, 0.0))
        return jnp.where(delta >= 0, fwd, bwd) * scale

    dmat0 = decay_mat(0)
    dmat1 = decay_mat(1)

    r2 = lax.broadcasted_iota(jnp.int32, (LANES, LANES), 0) // HEAD_DIM
    c2 = lax.broadcasted_iota(jnp.int32, (LANES, LANES), 1) // HEAD_DIM
    same_head = r2 == c2
    avg = jnp.where(same_head, 1.0 / HEAD_DIM, 0.0).astype(bf16)

    def chunk(ref, c):
        return ref[pl.ds(pl.multiple_of(c * C, C), C), :]

    def fwd_body(c, state):
        qc, kc, vc = chunk(q_ref, c), chunk(k_ref, c), chunk(v_ref, c)
        zero = jnp.zeros_like(qc)
        in0 = _nt_dot(jnp.where(lo, qc, zero), kc)
        in1 = _nt_dot(jnp.where(lo, zero, qc), kc)
        w = jnp.concatenate([(in0 * dmat0).astype(bf16), (in1 * dmat1).astype(bf16)], axis=1)
        vv = jnp.concatenate([jnp.where(lo, vc, zero), jnp.where(lo, zero, vc)], axis=0)
        intra = _dot(w, vv)
        qd = (qc.astype(f32) * qdec_f).astype(bf16)
        cross = _dot(qd, state.astype(bf16))
        acc_ref[pl.ds(pl.multiple_of(c * C, C), C), :] = intra + cross
        kd = (kc.astype(f32) * kdec_f).astype(bf16)
        upd = _tn_dot(kd, vc)
        return state * cdec_f + jnp.where(same_head, upd, 0.0)

    lax.fori_loop(0, n, fwd_body, jnp.zeros((LANES, LANES), f32), unroll=RET_UNROLL)

    def lane_mean(y):
        hi = y.astype(bf16)
        lo_part = (y - hi.astype(f32)).astype(bf16)
        return _dot(hi, avg) + _dot(lo_part, avg)

    def bwd_body(t, state):
        c = n - 1 - t
        qc, kc, vc = chunk(q_ref, c), chunk(k_ref, c), chunk(v_ref, c)
        qd = (qc.astype(f32) * qdec_b).astype(bf16)
        y = chunk(acc_ref, c) + _dot(qd, state.astype(bf16))
        y = y - lane_mean(y)
        y = y * lax.rsqrt(lane_mean(y * y) + NORM_EPS)
        gc = chunk(g_ref, c).astype(f32)
        out = jax.nn.silu(gc) * (y * nw_ref[...])
        o_ref[pl.ds(pl.multiple_of(c * C, C), C), :] = out.astype(o_ref.dtype)
        kd = (kc.astype(f32) * kdec_b).astype(bf16)
        upd = _tn_dot(kd, vc)
        return state * cdec_b + jnp.where(same_head, upd, 0.0)

    lax.fori_loop(0, n, bwd_body, jnp.zeros((LANES, LANES), f32), unroll=RET_UNROLL)


def retention(proj, lgf, lgb, norm_w, *, B, S):
    T = B * S
    blk0 = RET_COL0 // LANES
    npair = RET_HEADS // 2

    def col(seg):
        return lambda b, p: (b, blk0 + seg * npair + p)

    smem = pl.BlockSpec(memory_space=pltpu.SMEM)
    return pl.pallas_call(
        functools.partial(_retention_kernel, S=S),
        grid=(B, npair),
        in_specs=[smem, smem,
                  pl.BlockSpec((S, LANES), col(0)),
                  pl.BlockSpec((S, LANES), col(1)),
                  pl.BlockSpec((S, LANES), col(2)),
                  pl.BlockSpec((S, LANES), col(3)),
                  pl.BlockSpec((1, LANES), lambda b, p: (0, p))],
        out_specs=pl.BlockSpec((S, LANES), lambda b, p: (b, p)),
        out_shape=jax.ShapeDtypeStruct((T, RET_W), bf16),
        scratch_shapes=[pltpu.VMEM((S, LANES), f32)],
        compiler_params=pltpu.CompilerParams(
            dimension_semantics=("parallel", "parallel"), vmem_limit_bytes=VMEM_LIMIT),
        name="retention",
    )(lgf, lgb, proj, proj, proj, proj, norm_w.reshape(1, RET_W))


def _na_window_start(r, rows):
    return jnp.clip(r - NA_KH // 2, 0, rows - NA_KH)


def _na_kernel(q_ref, k_ref, v_ref, bias_ref, o_ref, *, rows):
    g = pl.program_id(1)
    nwin = NA_KH * GRID_W
    lane = lax.broadcasted_iota(jnp.int32, (1, LANES), 1)
    lo = lane < HEAD_DIM
    scale = HEAD_DIM ** -0.5
    npair = NA_HEADS // 2
    for j in range(NA_ROWS_PER_STEP):
        r = g * NA_ROWS_PER_STEP + j
        start = pl.multiple_of(_na_window_start(r, rows) * GRID_W, GRID_W)
        tok = slice(j * GRID_W, (j + 1) * GRID_W)
        parts = []
        for p in range(npair):
            cols = slice(p * LANES, (p + 1) * LANES)
            qp = (q_ref[tok, cols].astype(f32) * scale).astype(bf16)
            kp = k_ref[pl.ds(start, nwin), cols]
            zero = jnp.zeros_like(qp)
            parts.append(_nt_dot(jnp.where(lo, qp, zero), kp))
            parts.append(_nt_dot(jnp.where(lo, zero, qp), kp))
        d0 = NA_KH - 1 - (r - _na_window_start(r, rows))
        bias = jnp.concatenate(
            [jnp.concatenate([bias_ref[h, d0 + w] for w in range(0, NA_KH, 2)], axis=1)
             for h in range(NA_HEADS)], axis=0)
        s = jnp.concatenate(parts, axis=0) + bias
        e = jnp.exp(s - jnp.max(s, axis=-1, keepdims=True))
        inv = 1.0 / jnp.sum(e, axis=-1, keepdims=True)
        eb = e.astype(bf16)
        for p in range(npair):
            cols = slice(p * LANES, (p + 1) * LANES)
            vp = v_ref[pl.ds(start, nwin), cols]
            r0, r1, r2 = 2 * p * GRID_W, (2 * p + 1) * GRID_W, (2 * p + 2) * GRID_W
            even = _dot(eb[r0:r1], vp) * inv[r0:r1]
            odd = _dot(eb[r1:r2], vp) * inv[r1:r2]
            o_ref[tok, cols] = jnp.where(lo, even, odd).astype(o_ref.dtype)


def neighbourhood_attention(proj, bias_table, *, B, S):
    T = B * S
    rows = S // GRID_W
    nwin = NA_KH * GRID_W
    R = NA_ROWS_PER_STEP
    steps = rows // R
    assert rows % R == 0 and rows >= NA_KH

    return pl.pallas_call(
        functools.partial(_na_kernel, rows=rows),
        grid=(B, steps),
        in_specs=[pl.BlockSpec((R * GRID_W, NA_W), lambda b, g: (b * steps + g, 0)),
                  pl.BlockSpec((S, NA_W), lambda b, g: (b, 1)),
                  pl.BlockSpec((S, NA_W), lambda b, g: (b, 2)),
                  pl.BlockSpec(bias_table.shape, lambda b, g: (0, 0, 0, 0))],
        out_specs=pl.BlockSpec((R * GRID_W, NA_W), lambda b, g: (b * steps + g, 0)),
        out_shape=jax.ShapeDtypeStruct((T, NA_W), bf16),
        compiler_params=pltpu.CompilerParams(
            dimension_semantics=("parallel", "arbitrary"), vmem_limit_bytes=VMEM_LIMIT),
        name="neighbourhood_attention",
    )(proj, proj, proj, bias_table)


def na_bias_table(rpb):
    qcol = np.arange(GRID_W)[:, None]
    kcol = np.arange(GRID_W)[None, :]
    cs = np.clip(qcol - NA_KW // 2, 0, GRID_W - NA_KW)
    in_win = (kcol >= cs) & (kcol < cs + NA_KW)
    dc = np.clip(kcol - qcol + NA_KW - 1, 0, 2 * NA_KW - 2)
    blocks = jnp.where(jnp.asarray(in_win), rpb.astype(f32)[:, :, dc], NEG_INF)
    return jnp.concatenate([blocks[:, :-1], blocks[:, 1:]], axis=-1)


def _diff_position_features(pos, lane_in_half, first):
    f = lane_in_half - first
    lo = jnp.bitwise_and(pos, DIFF_POS_RADIX - 1)
    hi = pos - lo
    val = jnp.where(jnp.bitwise_and(f, 1) == 0, hi, lo)
    return jnp.where((f >= 0) & (f < 2 * DIFF_SLOPE_PARTS), val, 0).astype(f32)


def _diff_kernel(lam_ref, q_ref, qnext_ref, k_ref, v_ref, cvec_ref, nw_ref, o_ref,
                 kaug_ref, vaug_ref, qv_ref, s_even_ref, s_odd_ref, m_even_ref, m_odd_ref, *, S, out_scale):
    tq = DIFF_TQ
    nkt = S // tq
    qi = pl.program_id(2)
    lane = lax.broadcasted_iota(jnp.int32, (1, LANES), 1)
    lo = lane < HEAD_DIM
    lane_in_half = jnp.bitwise_and(lane, HEAD_DIM - 1)
    cq = cvec_ref[0:1, :]
    ck = cvec_ref[1:2, :]

    def key_rows(tile, r):
        t = jnp.bitwise_and(tile + r, nkt - 1)
        return pl.ds(pl.multiple_of(t * tq, tq), tq)

    def halves_max(x):
        return jnp.maximum(x[:, :LANES], x[:, LANES:])

    def pipeline(q_blk_ref, nxt_tile, nxt_s_ref, nxt_m_ref, cur_tile, cur_s_ref, cur_m_ref, out_ref):
        if q_blk_ref is not None:
            qs = (q_blk_ref[...].astype(f32) * (HEAD_DIM ** -0.5 * LOG2E)).astype(bf16)
            qpos = nxt_tile * tq + lax.broadcasted_iota(jnp.int32, (tq, 1), 0)
            fq = _diff_position_features(qpos, lane_in_half, 0) + cq
            zero = jnp.zeros_like(qs)
            for v, f in enumerate((fq.astype(bf16), zero, (-fq).astype(bf16))):
                qv_ref[v] = jnp.where(lo, qs, f)
                qv_ref[3 + v] = jnp.where(lo, f, qs)
            d = (lax.broadcasted_iota(jnp.int32, (tq, tq), 0)
                 - lax.broadcasted_iota(jnp.int32, (tq, tq), 1)).astype(f32)
            diag_bias = -cvec_ref[2:3, 0:1] * jnp.abs(d)

        if cur_s_ref is not None:
            m_cur = [jnp.max(cur_m_ref[c], axis=1, keepdims=True) for c in range(2)]
            acc = [None, None]
        m_acc = [None, None]
        for r in range(nkt):
            if cur_s_ref is not None:
                v_ones = vaug_ref[key_rows(cur_tile, r), :]
                for c in range(2):
                    e = jnp.exp2(cur_s_ref[c, r] - m_cur[c]).astype(bf16)
                    pv = _dot(e, v_ones)
                    acc[c] = pv if acc[c] is None else acc[c] + pv
            if q_blk_ref is None:
                continue
            rows = key_rows(nxt_tile, r)
            variant = 1 if r == 0 else jnp.where(nxt_tile + r >= nkt, 0, 2)
            for c in range(2):
                s = _nt_dot(qv_ref[3 * c + variant], kaug_ref[c, rows, :])
                if r == 0:
                    s = s + diag_bias
                nxt_s_ref[c, r] = s
                m_acc[c] = halves_max(s) if r == 0 else jnp.maximum(m_acc[c], halves_max(s))
        if q_blk_ref is not None:
            for c in range(2):
                nxt_m_ref[c] = m_acc[c]
        if cur_s_ref is None:
            return

        r1 = 1.0 / acc[0][:, DIFF_V_DIM:DIFF_V_DIM + 1]
        r2 = lam_ref[0] / acc[1][:, DIFF_V_DIM:DIFF_V_DIM + 1]
        out = acc[0][:, :DIFF_V_DIM] * r1 - acc[1][:, :DIFF_V_DIM] * r2
        y = out * lax.rsqrt(jnp.mean(out * out, axis=-1, keepdims=True) + NORM_EPS)
        out_ref[...] = (y * nw_ref[...] * out_scale).astype(out_ref.dtype)

    @pl.when(qi == 0)
    def _():
        kpos = lax.broadcasted_iota(jnp.int32, (S, 1), 0)
        fk = (_diff_position_features(kpos, lane_in_half, 2 * DIFF_SLOPE_PARTS) + ck).astype(bf16)
        k = k_ref[...]
        kaug_ref[0] = jnp.where(lo, k, fk)
        kaug_ref[1] = jnp.where(lo, fk, k)
        vaug_ref[:, 0:DIFF_V_DIM] = v_ref[...]
        vaug_ref[:, DIFF_V_DIM:] = jnp.broadcast_to(jnp.where(lane == 0, 1.0, 0.0).astype(bf16), (S, LANES))
        pipeline(q_ref.at[0:tq], 0, s_even_ref, m_even_ref, None, None, None, None)

    even = 2 * qi
    last = nkt // 2 - 1
    pipeline(q_ref.at[tq:2 * tq], even + 1, s_odd_ref, m_odd_ref, even, s_even_ref, m_even_ref, o_ref.at[0:tq])

    @pl.when(qi < last)
    def _():
        pipeline(qnext_ref, even + 2, s_even_ref, m_even_ref, even + 1, s_odd_ref, m_odd_ref,
                 o_ref.at[tq:2 * tq])

    @pl.when(qi == last)
    def _():
        pipeline(None, None, None, None, even + 1, s_odd_ref, m_odd_ref, o_ref.at[tq:2 * tq])


def diff_slope_vectors(slopes):
    c = slopes.astype(f32) * LOG2E
    parts = []
    rest = c
    for _ in range(DIFF_SLOPE_PARTS):
        p = rest.astype(bf16).astype(f32)
        parts.append(p)
        rest = rest - p
    parts = jnp.stack(parts, axis=1)
    lane_in_half = np.arange(LANES) % HEAD_DIM
    n = 2 * DIFF_SLOPE_PARTS
    q_sel = np.where((lane_in_half >= n) & (lane_in_half < 2 * n), (lane_in_half - n) // 2, -1)
    k_sel = np.where(lane_in_half < n, lane_in_half // 2, -1)
    zero = jnp.zeros((slopes.shape[0], 1), f32)
    padded = jnp.concatenate([parts, zero], axis=1)
    cq = padded[:, q_sel]
    ck = -padded[:, k_sel]
    return jnp.stack([cq, ck, jnp.broadcast_to(c[:, None], cq.shape)], axis=1)


def diff_attention(proj, cvec, lam, norm_w, *, B, S, lam_init):
    T = B * S
    tq = DIFF_TQ
    nq = S // tq
    assert S <= DIFF_POS_RADIX * DIFF_POS_RADIX and nq & (nq - 1) == 0 and nq % 2 == 0
    steps = nq // 2
    blk0 = DIFF_COL0 // LANES
    smem = pl.BlockSpec(memory_space=pltpu.SMEM)
    return pl.pallas_call(
        functools.partial(_diff_kernel, S=S, out_scale=1.0 - lam_init),
        grid=(B, DIFF_HEADS, steps),
        in_specs=[smem,
                  pl.BlockSpec((2 * tq, LANES), lambda b, h, i: (b * steps + i, blk0 + h)),
                  pl.BlockSpec((tq, LANES), lambda b, h, i: (b * nq + jnp.minimum(2 * i + 2, nq - 1), blk0 + h)),
                  pl.BlockSpec((S, LANES), lambda b, h, i: (b, blk0 + DIFF_HEADS + h)),
                  pl.BlockSpec((S, LANES), lambda b, h, i: (b, blk0 + 2 * DIFF_HEADS + h)),
                  pl.BlockSpec((None, 3, LANES), lambda b, h, i: (h, 0, 0)),
                  pl.BlockSpec((1, LANES), lambda b, h, i: (0, h))],
        out_specs=pl.BlockSpec((2 * tq, LANES), lambda b, h, i: (b * steps + i, h)),
        out_shape=jax.ShapeDtypeStruct((T, DIFF_W), bf16),
        scratch_shapes=[pltpu.VMEM((2, S, LANES), bf16),
                        pltpu.VMEM((S, 2 * LANES), bf16),
                        pltpu.VMEM((6, tq, LANES), bf16),
                        pltpu.VMEM((2, nq, tq, tq), f32),
                        pltpu.VMEM((2, nq, tq, tq), f32),
                        pltpu.VMEM((2, tq, LANES), f32),
                        pltpu.VMEM((2, tq, LANES), f32)],
        compiler_params=pltpu.CompilerParams(
            dimension_semantics=("parallel", "parallel", "arbitrary"), vmem_limit_bytes=VMEM_LIMIT),
        name="diff_attention",
    )(lam, proj, proj, proj, proj, cvec, norm_w.reshape(1, DIFF_W))


def _out_proj_kernel(x_ref, r_ref, n_ref, d_ref, w_ref, o_ref):
    acc = _dot(r_ref[...], w_ref[0:RET_W, :])
    acc += _dot(n_ref[...], w_ref[RET_W:RET_W + NA_W, :])
    acc += _dot(d_ref[...], w_ref[RET_W + NA_W:, :])
    o_ref[...] = x_ref[...] + acc


def out_proj(x, ret, na, diff, w_out, layer, *, tm=512):
    T, D = x.shape
    row = lambda i: (i, 0)
    return pl.pallas_call(
        _out_proj_kernel,
        grid=(T // tm,),
        in_specs=[pl.BlockSpec((tm, D), row),
                  pl.BlockSpec((tm, RET_W), row),
                  pl.BlockSpec((tm, NA_W), row),
                  pl.BlockSpec((tm, DIFF_W), row),
                  pl.BlockSpec((None, D, D), lambda i: (layer, 0, 0))],
        out_specs=pl.BlockSpec((tm, D), row),
        out_shape=jax.ShapeDtypeStruct((T, D), f32),
        compiler_params=pltpu.CompilerParams(
            dimension_semantics=("parallel",), vmem_limit_bytes=VMEM_LIMIT),
        name="out_proj",
    )(x, ret, na, diff, w_out)


def _rms(x):
    return x * lax.rsqrt(jnp.mean(x * x, axis=-1, keepdims=True) + NORM_EPS)


def _ffn_gate_kernel(x_ref, xp_ref, xn_ref, nw_ref, wa_ref, wg_ref, hw_ref, o_ref, h_ref, hh_ref,
                     *, S, tm):
    i = pl.program_id(0)

    @pl.when(pl.program_id(1) == 0)
    def _():
        nw = nw_ref[...]
        h_ref[...] = (_rms(x_ref[...]) * nw).astype(bf16)
        hh_ref[0:F32_SUBLANES, :] = (_rms(xp_ref[...]) * nw).astype(bf16)
        hh_ref[F32_SUBLANES:, :] = (_rms(xn_ref[...]) * nw).astype(bf16)

    h = h_ref[...]
    hh = hh_ref[...]
    row = lax.broadcasted_iota(jnp.int32, (tm, 1), 0)
    t0 = i * tm
    at_seq_start = (t0 % S) == 0
    at_seq_end = ((t0 + tm) % S) == 0
    for c0 in range(0, o_ref.shape[1], MXU_WIDTH):
        cols = slice(c0, c0 + MXU_WIDTH)
        wg = wg_ref[:, cols]
        g = _dot(h, wg)
        g_halo = _dot(hh, wg)
        a = _dot(h, wa_ref[:, cols])
        prev = jnp.where(at_seq_start, 0.0, g_halo[F32_SUBLANES - 1:F32_SUBLANES, :])
        nxt = jnp.where(at_seq_end, 0.0, g_halo[F32_SUBLANES:F32_SUBLANES + 1, :])
        g_up = jnp.where(row == 0, prev, pltpu.roll(g, 1, 0))
        g_dn = jnp.where(row == tm - 1, nxt, pltpu.roll(g, tm - 1, 0))
        half = g_up * hw_ref[0:1, cols] + g * hw_ref[1:2, cols] + g_dn * hw_ref[2:3, cols] + hw_ref[3:4, cols]
        u = half * (GELU_C1 + GELU_C3 * (half * half))
        ah = a * half
        o_ref[:, cols] = (ah + ah * jnp.tanh(u)).astype(o_ref.dtype)


def ffn_gate(x, nw, w_up, layer, conv_w, conv_b, *, S, tm=1024, tf=512):
    T, D = x.shape
    half_conv = 0.5 * jnp.concatenate([conv_w, conv_b.reshape(1, D_FF)], axis=0)
    nf = D_FF // tf
    rb = tm // F32_SUBLANES
    last_rb = T // F32_SUBLANES - 1
    assert S % tm == 0
    return pl.pallas_call(
        functools.partial(_ffn_gate_kernel, S=S, tm=tm),
        grid=(T // tm, nf),
        in_specs=[pl.BlockSpec((tm, D), lambda i, j: (i, 0)),
                  pl.BlockSpec((F32_SUBLANES, D), lambda i, j: (jnp.maximum(i * rb - 1, 0), 0)),
                  pl.BlockSpec((F32_SUBLANES, D), lambda i, j: (jnp.minimum((i + 1) * rb, last_rb), 0)),
                  pl.BlockSpec((1, D), lambda i, j: (0, 0)),
                  pl.BlockSpec((None, D, tf), lambda i, j: (layer, 0, j)),
                  pl.BlockSpec((None, D, tf), lambda i, j: (layer, 0, nf + j)),
                  pl.BlockSpec((4, tf), lambda i, j: (0, j))],
        out_specs=pl.BlockSpec((tm, tf), lambda i, j: (i, j)),
        out_shape=jax.ShapeDtypeStruct((T, D_FF), bf16),
        scratch_shapes=[pltpu.VMEM((tm, D), bf16), pltpu.VMEM((2 * F32_SUBLANES, D), bf16)],
        compiler_params=pltpu.CompilerParams(
            dimension_semantics=("parallel", "arbitrary"), vmem_limit_bytes=VMEM_LIMIT),
        name="ffn_gate",
    )(x, x, x, nw.reshape(1, D), w_up, w_up, half_conv)


def _ffn_down_kernel(x_ref, g_ref, wd_ref, fw_ref, o_ref, *, final_norm):
    k = pl.program_id(1)

    @pl.when(k == 0)
    def _():
        o_ref[...] = x_ref[...]

    o_ref[...] += _dot(g_ref[...], wd_ref[...])

    if final_norm:
        @pl.when(k == pl.num_programs(1) - 1)
        def _():
            x = o_ref[...]
            y = x * lax.rsqrt(jnp.mean(x * x, axis=-1, keepdims=True) + NORM_EPS)
            o_ref[...] = y * fw_ref[...]


def ffn_down(x, gated, w_down, layer, final_w, *, final_norm, tm=512, tk=2816):
    T, D = x.shape
    nk = D_FF // tk
    return pl.pallas_call(
        functools.partial(_ffn_down_kernel, final_norm=final_norm),
        grid=(T // tm, nk),
        in_specs=[pl.BlockSpec((tm, D), lambda i, k: (i, 0)),
                  pl.BlockSpec((tm, tk), lambda i, k: (i, k)),
                  pl.BlockSpec((None, tk, D), lambda i, k: (layer, k, 0)),
                  pl.BlockSpec((1, D), lambda i, k: (0, 0))],
        out_specs=pl.BlockSpec((tm, D), lambda i, k: (i, 0)),
        out_shape=jax.ShapeDtypeStruct((T, D), f32),
        compiler_params=pltpu.CompilerParams(
            dimension_semantics=("parallel", "arbitrary"), vmem_limit_bytes=VMEM_LIMIT),
        name="ffn_down",
    )(x, gated, w_down, final_w.reshape(1, D))


def _prepare_layer(l, norm1_w, ret_decay_fwd, ret_decay_bwd, ret_norm_w, na_rpb,
                   lq1, lk1, lq2, lk2, diff_norm_w, norm2_w, ffn_conv_w, ffn_conv_b):
    lam_init = 0.8 - 0.6 * math.exp(-0.3 * l)
    lam = (jnp.exp(jnp.sum(lq1[l].astype(f32) * lk1[l].astype(f32)))
           - jnp.exp(jnp.sum(lq2[l].astype(f32) * lk2[l].astype(f32))) + lam_init)
    return dict(
        norm1_w=norm1_w[l].astype(f32),
        lgf=jax.nn.log_sigmoid(ret_decay_fwd[l].astype(f32)),
        lgb=jax.nn.log_sigmoid(ret_decay_bwd[l].astype(f32)),
        ret_norm_w=ret_norm_w[l].astype(f32),
        na_bias=na_bias_table(na_rpb[l]),
        lam=lam.reshape(1), lam_init=lam_init, diff_norm_w=diff_norm_w[l].astype(f32),
        norm2_w=norm2_w[l].astype(f32), conv_w=ffn_conv_w[l].astype(f32), conv_b=ffn_conv_b[l].astype(f32))


def _prepare_weights(w_in, w_out, ffn_w_up, ffn_w_down):
    n0, d0 = 4 * RET_W, 4 * RET_W + 3 * NA_W
    w_in_perm = jnp.concatenate([w_in[:, :, n0:d0], w_in[:, :, d0:], w_in[:, :, :n0]], axis=2)
    return dict(w_in=w_in_perm.astype(bf16), w_out=w_out.astype(bf16),
                w_up=ffn_w_up.astype(bf16), w_down=ffn_w_down.astype(bf16))


def _trunk(x, layers, weights, final_norm_w, slope_vecs):
    B, S, D = x.shape
    x = x.reshape(B * S, D)
    for l, p in enumerate(layers):
        proj = norm_matmul(x, p["norm1_w"], weights["w_in"], l)
        ret = retention(proj, p["lgf"], p["lgb"], p["ret_norm_w"], B=B, S=S)
        na = neighbourhood_attention(proj, p["na_bias"], B=B, S=S)
        diff = diff_attention(proj, slope_vecs, p["lam"], p["diff_norm_w"], B=B, S=S, lam_init=p["lam_init"])
        x = out_proj(x, ret, na, diff, weights["w_out"], l)
        gated = ffn_gate(x, p["norm2_w"], weights["w_up"], l, p["conv_w"], p["conv_b"], S=S)
        x = ffn_down(x, gated, weights["w_down"], l, final_norm_w.astype(f32),
                     final_norm=(l == len(layers) - 1))
    return x.reshape(B, S, D)


def kernel(x_prompt, x_sample, norm1_w, w_in, ret_decay_fwd, ret_decay_bwd, ret_norm_w, na_rpb, diff_lambda_q1, diff_lambda_k1, diff_lambda_q2, diff_lambda_k2, diff_norm_w, w_out, norm2_w, ffn_w_up, ffn_conv_w, ffn_conv_b, ffn_w_down, final_norm_w):
    layers = [_prepare_layer(l, norm1_w, ret_decay_fwd, ret_decay_bwd, ret_norm_w, na_rpb,
                             diff_lambda_q1, diff_lambda_k1, diff_lambda_q2, diff_lambda_k2,
                             diff_norm_w, norm2_w, ffn_conv_w, ffn_conv_b)
              for l in range(w_in.shape[0])]
    weights = _prepare_weights(w_in, w_out, ffn_w_up, ffn_w_down)
    slopes = 2.0 ** (-8.0 * (jnp.arange(DIFF_HEADS, dtype=f32) + 1.0) / DIFF_HEADS)
    slope_vecs = diff_slope_vectors(slopes)
    y_prompt = _trunk(x_prompt, layers, weights, final_norm_w, slope_vecs)
    y_sample = _trunk(x_sample, layers, weights, final_norm_w, slope_vecs)
    return (y_prompt, y_sample)
```

```python
import functools
import math

import jax
import jax.numpy as jnp
import numpy as np
from jax import lax
from jax.experimental import pallas as pl
from jax.experimental.pallas import tpu as pltpu

D_MODEL = 2048
DEPTH = 2
HEAD_DIM = 64
RET_HEADS = 8
RET_W = RET_HEADS * HEAD_DIM
NA_HEADS = 12
NA_W = NA_HEADS * HEAD_DIM
GRID_W = 64
NA_KH = 8
NA_KW = 16
DIFF_HEADS = 6
DIFF_V_DIM = 2 * HEAD_DIM
DIFF_W = DIFF_HEADS * DIFF_V_DIM
D_IN = 4 * RET_W + 3 * NA_W + 3 * DIFF_W
D_FF = 5632
NORM_EPS = 1e-6
NEG_INF = -1e30

LANES = 128
F32_SUBLANES = 8
MXU_WIDTH = 256
RET_CHUNK = 256
RET_UNROLL = 8
FFN_ROW_PIECES = 8
NA_ROWS_PER_STEP = 4
DIFF_TQ = 256
DIFF_POS_RADIX = 64
DIFF_SLOPE_PARTS = 3
LOG2E = 1.4426950408889634
GELU_C1 = 2.0 * math.sqrt(2.0 / math.pi)
GELU_C3 = 8.0 * math.sqrt(2.0 / math.pi) * 0.044715
VMEM_LIMIT = 56 * 1024 * 1024

NA_COL0 = 0
DIFF_COL0 = 3 * NA_W
RET_COL0 = 3 * NA_W + 3 * DIFF_W

f32 = jnp.float32
bf16 = jnp.bfloat16


def _nt_dot(a, b):
    return lax.dot_general(a, b, (((1,), (1,)), ((), ())), preferred_element_type=f32)


def _tn_dot(a, b):
    return lax.dot_general(a, b, (((0,), (0,)), ((), ())), preferred_element_type=f32)


def _dot(a, b):
    return jnp.dot(a, b, preferred_element_type=f32)


def _norm_matmul_kernel(x_ref, nw_ref, w_ref, o_ref, h_ref):
    @pl.when(pl.program_id(1) == 0)
    def _():
        x = x_ref[...]
        y = x * lax.rsqrt(jnp.mean(x * x, axis=-1, keepdims=True) + NORM_EPS)
        h_ref[...] = (y * nw_ref[...]).astype(bf16)

    o_ref[...] = _dot(h_ref[...], w_ref[...]).astype(o_ref.dtype)


def norm_matmul(x, nw, w, layer, *, tm=512, tn=3328):
    T, D = x.shape
    N = w.shape[2]
    return pl.pallas_call(
        _norm_matmul_kernel,
        grid=(T // tm, N // tn),
        in_specs=[
            pl.BlockSpec((tm, D), lambda i, j: (i, 0)),
            pl.BlockSpec((1, D), lambda i, j: (0, 0)),
            pl.BlockSpec((None, D, tn), lambda i, j: (layer, 0, j)),
        ],
        out_specs=pl.BlockSpec((tm, tn), lambda i, j: (i, j)),
        out_shape=jax.ShapeDtypeStruct((T, N), bf16),
        scratch_shapes=[pltpu.VMEM((tm, D), bf16)],
        compiler_params=pltpu.CompilerParams(
            dimension_semantics=("parallel", "arbitrary"), vmem_limit_bytes=VMEM_LIMIT),
        name="norm_matmul",
    )(x, nw.reshape(1, D), w)


def _retention_kernel(lgf_ref, lgb_ref, q_ref, k_ref, v_ref, g_ref, nw_ref, o_ref, acc_ref, *, S):
    C = RET_CHUNK
    n = S // C
    pair = pl.program_id(1)
    scale = HEAD_DIM ** -0.5

    lane = lax.broadcasted_iota(jnp.int32, (1, LANES), 1)
    lo = lane < HEAD_DIM
    lgf = jnp.where(lo, lgf_ref[2 * pair], lgf_ref[2 * pair + 1])
    lgb = jnp.where(lo, lgb_ref[2 * pair], lgb_ref[2 * pair + 1])

    pos = lax.broadcasted_iota(jnp.int32, (C, 1), 0).astype(f32)
    qdec_f = jnp.exp(lgf * (pos + 1.0))
    kdec_f = jnp.exp(lgf * (C - 1.0 - pos)) * scale
    cdec_f = jnp.exp(lgf * float(C))
    qdec_b = jnp.exp(lgb * (C - pos))
    kdec_b = jnp.exp(lgb * pos) * scale
    cdec_b = jnp.exp(lgb * float(C))

    ri = lax.broadcasted_iota(jnp.int32, (C, C), 0)
    ci = lax.broadcasted_iota(jnp.int32, (C, C), 1)
    delta = (ri - ci).astype(f32)

    def decay_mat(h):
        fwd = jnp.exp(lgf_ref[2 * pair + h] * jnp.maximum(delta, 0.0))
        bwd = jnp.exp(lgb_ref[2 * pair + h] * jnp.maximum(-delta, 0.0))
        return jnp.where(delta >= 0, fwd, bwd) * scale

    dmat0 = decay_mat(0)
    dmat1 = decay_mat(1)

    r2 = lax.broadcasted_iota(jnp.int32, (LANES, LANES), 0) // HEAD_DIM
    c2 = lax.broadcasted_iota(jnp.int32, (LANES, LANES), 1) // HEAD_DIM
    same_head = r2 == c2
    avg = jnp.where(same_head, 1.0 / HEAD_DIM, 0.0).astype(bf16)

    def chunk(ref, c):
        return ref[pl.ds(pl.multiple_of(c * C, C), C), :]

    def fwd_body(c, state):
        qc, kc, vc = chunk(q_ref, c), chunk(k_ref, c), chunk(v_ref, c)
        zero = jnp.zeros_like(qc)
        in0 = _nt_dot(jnp.where(lo, qc, zero), kc)
        in1 = _nt_dot(jnp.where(lo, zero, qc), kc)
        w = jnp.concatenate([(in0 * dmat0).astype(bf16), (in1 * dmat1).astype(bf16)], axis=1)
        vv = jnp.concatenate([jnp.where(lo, vc, zero), jnp.where(lo, zero, vc)], axis=0)
        intra = _dot(w, vv)
        qd = (qc.astype(f32) * qdec_f).astype(bf16)
        cross = _dot(qd, state.astype(bf16))
        acc_ref[pl.ds(pl.multiple_of(c * C, C), C), :] = intra + cross
        kd = (kc.astype(f32) * kdec_f).astype(bf16)
        upd = _tn_dot(kd, vc)
        return state * cdec_f + jnp.where(same_head, upd, 0.0)

    lax.fori_loop(0, n, fwd_body, jnp.zeros((LANES, LANES), f32), unroll=RET_UNROLL)

    def lane_mean(y):
        hi = y.astype(bf16)
        lo_part = (y - hi.astype(f32)).astype(bf16)
        return _dot(hi, avg) + _dot(lo_part, avg)

    def bwd_body(t, state):
        c = n - 1 - t
        qc, kc, vc = chunk(q_ref, c), chunk(k_ref, c), chunk(v_ref, c)
        qd = (qc.astype(f32) * qdec_b).astype(bf16)
        y = chunk(acc_ref, c) + _dot(qd, state.astype(bf16))
        y = y - lane_mean(y)
        y = y * lax.rsqrt(lane_mean(y * y) + NORM_EPS)
        gc = chunk(g_ref, c).astype(f32)
        out = jax.nn.silu(gc) * (y * nw_ref[...])
        o_ref[pl.ds(pl.multiple_of(c * C, C), C), :] = out.astype(o_ref.dtype)
        kd = (kc.astype(f32) * kdec_b).astype(bf16)
        upd = _tn_dot(kd, vc)
        return state * cdec_b + jnp.where(same_head, upd, 0.0)

    lax.fori_loop(0, n, bwd_body, jnp.zeros((LANES, LANES), f32), unroll=RET_UNROLL)


def retention(proj, lgf, lgb, norm_w, *, B, S):
    T = B * S
    blk0 = RET_COL0 // LANES
    npair = RET_HEADS // 2

    def col(seg):
        return lambda b, p: (b, blk0 + seg * npair + p)

    smem = pl.BlockSpec(memory_space=pltpu.SMEM)
    return pl.pallas_call(
        functools.partial(_retention_kernel, S=S),
        grid=(B, npair),
        in_specs=[smem, smem,
                  pl.BlockSpec((S, LANES), col(0)),
                  pl.BlockSpec((S, LANES), col(1)),
                  pl.BlockSpec((S, LANES), col(2)),
                  pl.BlockSpec((S, LANES), col(3)),
                  pl.BlockSpec((1, LANES), lambda b, p: (0, p))],
        out_specs=pl.BlockSpec((S, LANES), lambda b, p: (b, p)),
        out_shape=jax.ShapeDtypeStruct((T, RET_W), bf16),
        scratch_shapes=[pltpu.VMEM((S, LANES), f32)],
        compiler_params=pltpu.CompilerParams(
            dimension_semantics=("parallel", "parallel"), vmem_limit_bytes=VMEM_LIMIT),
        name="retention",
    )(lgf, lgb, proj, proj, proj, proj, norm_w.reshape(1, RET_W))


def _na_window_start(r, rows):
    return jnp.clip(r - NA_KH // 2, 0, rows - NA_KH)


def _na_kernel(q_ref, k_ref, v_ref, bias_ref, o_ref, *, rows):
    g = pl.program_id(1)
    nwin = NA_KH * GRID_W
    lane = lax.broadcasted_iota(jnp.int32, (1, LANES), 1)
    lo = lane < HEAD_DIM
    scale = HEAD_DIM ** -0.5
    npair = NA_HEADS // 2
    for j in range(NA_ROWS_PER_STEP):
        r = g * NA_ROWS_PER_STEP + j
        start = pl.multiple_of(_na_window_start(r, rows) * GRID_W, GRID_W)
        tok = slice(j * GRID_W, (j + 1) * GRID_W)
        parts = []
        for p in range(npair):
            cols = slice(p * LANES, (p + 1) * LANES)
            qp = (q_ref[tok, cols].astype(f32) * scale).astype(bf16)
            kp = k_ref[pl.ds(start, nwin), cols]
            zero = jnp.zeros_like(qp)
            parts.append(_nt_dot(jnp.where(lo, qp, zero), kp))
            parts.append(_nt_dot(jnp.where(lo, zero, qp), kp))
        d0 = NA_KH - 1 - (r - _na_window_start(r, rows))
        bias = jnp.concatenate(
            [jnp.concatenate([bias_ref[h, d0 + w] for w in range(0, NA_KH, 2)], axis=1)
             for h in range(NA_HEADS)], axis=0)
        s = jnp.concatenate(parts, axis=0) + bias
        e = jnp.exp(s - jnp.max(s, axis=-1, keepdims=True))
        inv = 1.0 / jnp.sum(e, axis=-1, keepdims=True)
        eb = e.astype(bf16)
        for p in range(npair):
            cols = slice(p * LANES, (p + 1) * LANES)
            vp = v_ref[pl.ds(start, nwin), cols]
            r0, r1, r2 = 2 * p * GRID_W, (2 * p + 1) * GRID_W, (2 * p + 2) * GRID_W
            even = _dot(eb[r0:r1], vp) * inv[r0:r1]
            odd = _dot(eb[r1:r2], vp) * inv[r1:r2]
            o_ref[tok, cols] = jnp.where(lo, even, odd).astype(o_ref.dtype)


def neighbourhood_attention(proj, bias_table, *, B, S):
    T = B * S
    rows = S // GRID_W
    nwin = NA_KH * GRID_W
    R = NA_ROWS_PER_STEP
    steps = rows // R
    assert rows % R == 0 and rows >= NA_KH

    return pl.pallas_call(
        functools.partial(_na_kernel, rows=rows),
        grid=(B, steps),
        in_specs=[pl.BlockSpec((R * GRID_W, NA_W), lambda b, g: (b * steps + g, 0)),
                  pl.BlockSpec((S, NA_W), lambda b, g: (b, 1)),
                  pl.BlockSpec((S, NA_W), lambda b, g: (b, 2)),
                  pl.BlockSpec(bias_table.shape, lambda b, g: (0, 0, 0, 0))],
        out_specs=pl.BlockSpec((R * GRID_W, NA_W), lambda b, g: (b * steps + g, 0)),
        out_shape=jax.ShapeDtypeStruct((T, NA_W), bf16),
        compiler_params=pltpu.CompilerParams(
            dimension_semantics=("parallel", "arbitrary"), vmem_limit_bytes=VMEM_LIMIT),
        name="neighbourhood_attention",
    )(proj, proj, proj, bias_table)


def na_bias_table(rpb):
    qcol = np.arange(GRID_W)[:, None]
    kcol = np.arange(GRID_W)[None, :]
    cs = np.clip(qcol - NA_KW // 2, 0, GRID_W - NA_KW)
    in_win = (kcol >= cs) & (kcol < cs + NA_KW)
    dc = np.clip(kcol - qcol + NA_KW - 1, 0, 2 * NA_KW - 2)
    blocks = jnp.where(jnp.asarray(in_win), rpb.astype(f32)[:, :, dc], NEG_INF)
    return jnp.concatenate([blocks[:, :-1], blocks[:, 1:]], axis=-1)


def _diff_position_features(pos, lane_in_half, first):
    f = lane_in_half - first
    lo = jnp.bitwise_and(pos, DIFF_POS_RADIX - 1)
    hi = pos - lo
    val = jnp.where(jnp.bitwise_and(f, 1) == 0, hi, lo)
    return jnp.where((f >= 0) & (f < 2 * DIFF_SLOPE_PARTS), val, 0).astype(f32)


def _diff_kernel(lam_ref, q_ref, qnext_ref, k_ref, v_ref, cvec_ref, nw_ref, o_ref,
                 kaug_ref, vaug_ref, qv_ref, s_even_ref, s_odd_ref, m_even_ref, m_odd_ref, *, S, out_scale):
    tq = DIFF_TQ
    nkt = S // tq
    qi = pl.program_id(2)
    lane = lax.broadcasted_iota(jnp.int32, (1, LANES), 1)
    lo = lane < HEAD_DIM
    lane_in_half = jnp.bitwise_and(lane, HEAD_DIM - 1)
    cq = cvec_ref[0:1, :]
    ck = cvec_ref[1:2, :]

    def key_rows(tile, r):
        t = jnp.bitwise_and(tile + r, nkt - 1)
        return pl.ds(pl.multiple_of(t * tq, tq), tq)

    def halves_max(x):
        return jnp.maximum(x[:, :LANES], x[:, LANES:])

    def pipeline(q_blk_ref, nxt_tile, nxt_s_ref, nxt_m_ref, cur_tile, cur_s_ref, cur_m_ref, out_ref):
        if q_blk_ref is not None:
            qs = (q_blk_ref[...].astype(f32) * (HEAD_DIM ** -0.5 * LOG2E)).astype(bf16)
            qpos = nxt_tile * tq + lax.broadcasted_iota(jnp.int32, (tq, 1), 0)
            fq = _diff_position_features(qpos, lane_in_half, 0) + cq
            zero = jnp.zeros_like(qs)
            for v, f in enumerate((fq.astype(bf16), zero, (-fq).astype(bf16))):
                qv_ref[v] = jnp.where(lo, qs, f)
                qv_ref[3 + v] = jnp.where(lo, f, qs)
            d = (lax.broadcasted_iota(jnp.int32, (tq, tq), 0)
                 - lax.broadcasted_iota(jnp.int32, (tq, tq), 1)).astype(f32)
            diag_bias = -cvec_ref[2:3, 0:1] * jnp.abs(d)

        if cur_s_ref is not None:
            m_cur = [jnp.max(cur_m_ref[c], axis=1, keepdims=True) for c in range(2)]
            acc = [None, None]
        m_acc = [None, None]
        for r in range(nkt):
            if cur_s_ref is not None:
                v_ones = vaug_ref[key_rows(cur_tile, r), :]
                for c in range(2):
                    e = jnp.exp2(cur_s_ref[c, r] - m_cur[c]).astype(bf16)
                    pv = _dot(e, v_ones)
                    acc[c] = pv if acc[c] is None else acc[c] + pv
            if q_blk_ref is None:
                continue
            rows = key_rows(nxt_tile, r)
            variant = 1 if r == 0 else jnp.where(nxt_tile + r >= nkt, 0, 2)
            for c in range(2):
                s = _nt_dot(qv_ref[3 * c + variant], kaug_ref[c, rows, :])
                if r == 0:
                    s = s + diag_bias
                nxt_s_ref[c, r] = s
                m_acc[c] = halves_max(s) if r == 0 else jnp.maximum(m_acc[c], halves_max(s))
        if q_blk_ref is not None:
            for c in range(2):
                nxt_m_ref[c] = m_acc[c]
        if cur_s_ref is None:
            return

        r1 = 1.0 / acc[0][:, DIFF_V_DIM:DIFF_V_DIM + 1]
        r2 = lam_ref[0] / acc[1][:, DIFF_V_DIM:DIFF_V_DIM + 1]
        out = acc[0][:, :DIFF_V_DIM] * r1 - acc[1][:, :DIFF_V_DIM] * r2
        y = out * lax.rsqrt(jnp.mean(out * out, axis=-1, keepdims=True) + NORM_EPS)
        out_ref[...] = (y * nw_ref[...] * out_scale).astype(out_ref.dtype)

    @pl.when(qi == 0)
    def _():
        kpos = lax.broadcasted_iota(jnp.int32, (S, 1), 0)
        fk = (_diff_position_features(kpos, lane_in_half, 2 * DIFF_SLOPE_PARTS) + ck).astype(bf16)
        k = k_ref[...]
        kaug_ref[0] = jnp.where(lo, k, fk)
        kaug_ref[1] = jnp.where(lo, fk, k)
        vaug_ref[:, 0:DIFF_V_DIM] = v_ref[...]
        vaug_ref[:, DIFF_V_DIM:] = jnp.broadcast_to(jnp.where(lane == 0, 1.0, 0.0).astype(bf16), (S, LANES))
        pipeline(q_ref.at[0:tq], 0, s_even_ref, m_even_ref, None, None, None, None)

    even = 2 * qi
    last = nkt // 2 - 1
    pipeline(q_ref.at[tq:2 * tq], even + 1, s_odd_ref, m_odd_ref, even, s_even_ref, m_even_ref, o_ref.at[0:tq])

    @pl.when(qi < last)
    def _():
        pipeline(qnext_ref, even + 2, s_even_ref, m_even_ref, even + 1, s_odd_ref, m_odd_ref,
                 o_ref.at[tq:2 * tq])

    @pl.when(qi == last)
    def _():
        pipeline(None, None, None, None, even + 1, s_odd_ref, m_odd_ref, o_ref.at[tq:2 * tq])


def diff_slope_vectors(slopes):
    c = slopes.astype(f32) * LOG2E
    parts = []
    rest = c
    for _ in range(DIFF_SLOPE_PARTS):
        p = rest.astype(bf16).astype(f32)
        parts.append(p)
        rest = rest - p
    parts = jnp.stack(parts, axis=1)
    lane_in_half = np.arange(LANES) % HEAD_DIM
    n = 2 * DIFF_SLOPE_PARTS
    q_sel = np.where((lane_in_half >= n) & (lane_in_half < 2 * n), (lane_in_half - n) // 2, -1)
    k_sel = np.where(lane_in_half < n, lane_in_half // 2, -1)
    zero = jnp.zeros((slopes.shape[0], 1), f32)
    padded = jnp.concatenate([parts, zero], axis=1)
    cq = padded[:, q_sel]
    ck = -padded[:, k_sel]
    return jnp.stack([cq, ck, jnp.broadcast_to(c[:, None], cq.shape)], axis=1)


def diff_attention(proj, cvec, lam, norm_w, *, B, S, lam_init):
    T = B * S
    tq = DIFF_TQ
    nq = S // tq
    assert S <= DIFF_POS_RADIX * DIFF_POS_RADIX and nq & (nq - 1) == 0 and nq % 2 == 0
    steps = nq // 2
    blk0 = DIFF_COL0 // LANES
    smem = pl.BlockSpec(memory_space=pltpu.SMEM)
    return pl.pallas_call(
        functools.partial(_diff_kernel, S=S, out_scale=1.0 - lam_init),
        grid=(B, DIFF_HEADS, steps),
        in_specs=[smem,
                  pl.BlockSpec((2 * tq, LANES), lambda b, h, i: (b * steps + i, blk0 + h)),
                  pl.BlockSpec((tq, LANES), lambda b, h, i: (b * nq + jnp.minimum(2 * i + 2, nq - 1), blk0 + h)),
                  pl.BlockSpec((S, LANES), lambda b, h, i: (b, blk0 + DIFF_HEADS + h)),
                  pl.BlockSpec((S, LANES), lambda b, h, i: (b, blk0 + 2 * DIFF_HEADS + h)),
                  pl.BlockSpec((None, 3, LANES), lambda b, h, i: (h, 0, 0)),
                  pl.BlockSpec((1, LANES), lambda b, h, i: (0, h))],
        out_specs=pl.BlockSpec((2 * tq, LANES), lambda b, h, i: (b * steps + i, h)),
        out_shape=jax.ShapeDtypeStruct((T, DIFF_W), bf16),
        scratch_shapes=[pltpu.VMEM((2, S, LANES), bf16),
                        pltpu.VMEM((S, 2 * LANES), bf16),
                        pltpu.VMEM((6, tq, LANES), bf16),
                        pltpu.VMEM((2, nq, tq, tq), f32),
                        pltpu.VMEM((2, nq, tq, tq), f32),
                        pltpu.VMEM((2, tq, LANES), f32),
                        pltpu.VMEM((2, tq, LANES), f32)],
        compiler_params=pltpu.CompilerParams(
            dimension_semantics=("parallel", "parallel", "arbitrary"), vmem_limit_bytes=VMEM_LIMIT),
        name="diff_attention",
    )(lam, proj, proj, proj, proj, cvec, norm_w.reshape(1, DIFF_W))


def _out_proj_kernel(x_ref, r_ref, n_ref, d_ref, w_ref, o_ref):
    acc = _dot(r_ref[...], w_ref[0:RET_W, :])
    acc += _dot(n_ref[...], w_ref[RET_W:RET_W + NA_W, :])
    acc += _dot(d_ref[...], w_ref[RET_W + NA_W:, :])
    o_ref[...] = x_ref[...] + acc


def out_proj(x, ret, na, diff, w_out, layer, *, tm=512):
    T, D = x.shape
    row = lambda i: (i, 0)
    return pl.pallas_call(
        _out_proj_kernel,
        grid=(T // tm,),
        in_specs=[pl.BlockSpec((tm, D), row),
                  pl.BlockSpec((tm, RET_W), row),
                  pl.BlockSpec((tm, NA_W), row),
                  pl.BlockSpec((tm, DIFF_W), row),
                  pl.BlockSpec((None, D, D), lambda i: (layer, 0, 0))],
        out_specs=pl.BlockSpec((tm, D), row),
        out_shape=jax.ShapeDtypeStruct((T, D), f32),
        compiler_params=pltpu.CompilerParams(
            dimension_semantics=("parallel",), vmem_limit_bytes=VMEM_LIMIT),
        name="out_proj",
    )(x, ret, na, diff, w_out)


def _rms(x):
    return x * lax.rsqrt(jnp.mean(x * x, axis=-1, keepdims=True) + NORM_EPS)


def _ffn_gate_kernel(x_ref, xp_ref, xn_ref, nw_ref, wa_ref, wg_ref, hw_ref, o_ref, h_ref, hh_ref,
                     *, S, tm):
    i = pl.program_id(0)

    @pl.when(pl.program_id(1) == 0)
    def _():
        nw = nw_ref[...]
        h_ref[...] = (_rms(x_ref[...]) * nw).astype(bf16)
        hh_ref[0:F32_SUBLANES, :] = (_rms(xp_ref[...]) * nw).astype(bf16)
        hh_ref[F32_SUBLANES:, :] = (_rms(xn_ref[...]) * nw).astype(bf16)

    hh = hh_ref[...]
    t0 = i * tm
    at_seq_start = (t0 % S) == 0
    at_seq_end = ((t0 + tm) % S) == 0
    pm = tm // FFN_ROW_PIECES
    edge = 2 * F32_SUBLANES
    span = edge + F32_SUBLANES
    row = lax.broadcasted_iota(jnp.int32, (pm, 1), 0)
    seam_row = lax.broadcasted_iota(jnp.int32, (span, 1), 0)

    def gate(g_up, g_mid, g_dn, a, cols):
        half = g_up * hw_ref[0:1, cols] + g_mid * hw_ref[1:2, cols] + g_dn * hw_ref[2:3, cols] + hw_ref[3:4, cols]
        u = half * (GELU_C1 + GELU_C3 * (half * half))
        ah = a * half
        return ah + ah * jnp.tanh(u)

    for c0 in range(0, o_ref.shape[1], MXU_WIDTH):
        cols = slice(c0, c0 + MXU_WIDTH)
        wg = wg_ref[:, cols]
        wa = wa_ref[:, cols]
        g_halo = _dot(hh, wg)
        above = jnp.where(at_seq_start, 0.0, g_halo[F32_SUBLANES - 1:F32_SUBLANES, :])
        below_tile = jnp.where(at_seq_end, 0.0, g_halo[F32_SUBLANES:F32_SUBLANES + 1, :])
        held = None
        for p in range(FFN_ROW_PIECES):
            r0 = p * pm
            g = _dot(h_ref[r0:r0 + pm, :], wg)
            a = _dot(h_ref[r0:r0 + pm, :], wa)
            if held is not None:
                g_seam, a_seam, seam_at = held
                dn = jnp.where(seam_row == span - 1, g[0:1, :], pltpu.roll(g_seam, span - 1, 0))
                seam = gate(pltpu.roll(g_seam, 1, 0), g_seam, dn, a_seam, cols)
                o_ref[seam_at - edge:seam_at, cols] = seam[span - edge:span].astype(o_ref.dtype)
            up = jnp.where(row == 0, above, pltpu.roll(g, 1, 0))
            dn = pltpu.roll(g, pm - 1, 0)
            if p == FFN_ROW_PIECES - 1:
                dn = jnp.where(row == pm - 1, below_tile, dn)
            o_ref[r0:r0 + pm, cols] = gate(up, g, dn, a, cols).astype(o_ref.dtype)
            above = g[pm - 1:pm, :]
            held = (g[pm - span:pm, :], a[pm - span:pm, :], r0 + pm)


def ffn_gate(x, nw, w_up, layer, conv_w, conv_b, *, S, tm=1024, tf=512):
    T, D = x.shape
    half_conv = 0.5 * jnp.concatenate([conv_w, conv_b.reshape(1, D_FF)], axis=0)
    nf = D_FF // tf
    rb = tm // F32_SUBLANES
    last_rb = T // F32_SUBLANES - 1
    assert S % tm == 0
    return pl.pallas_call(
        functools.partial(_ffn_gate_kernel, S=S, tm=tm),
        grid=(T // tm, nf),
        in_specs=[pl.BlockSpec((tm, D), lambda i, j: (i, 0)),
                  pl.BlockSpec((F32_SUBLANES, D), lambda i, j: (jnp.maximum(i * rb - 1, 0), 0)),
                  pl.BlockSpec((F32_SUBLANES, D), lambda i, j: (jnp.minimum((i + 1) * rb, last_rb), 0)),
                  pl.BlockSpec((1, D), lambda i, j: (0, 0)),
                  pl.BlockSpec((None, D, tf), lambda i, j: (layer, 0, j)),
                  pl.BlockSpec((None, D, tf), lambda i, j: (layer, 0, nf + j)),
                  pl.BlockSpec((4, tf), lambda i, j: (0, j))],
        out_specs=pl.BlockSpec((tm, tf), lambda i, j: (i, j)),
        out_shape=jax.ShapeDtypeStruct((T, D_FF), bf16),
        scratch_shapes=[pltpu.VMEM((tm, D), bf16), pltpu.VMEM((2 * F32_SUBLANES, D), bf16)],
        compiler_params=pltpu.CompilerParams(
            dimension_semantics=("parallel", "arbitrary"), vmem_limit_bytes=VMEM_LIMIT),
        name="ffn_gate",
    )(x, x, x, nw.reshape(1, D), w_up, w_up, half_conv)


def _ffn_down_kernel(x_ref, g_ref, wd_ref, fw_ref, o_ref, *, final_norm):
    k = pl.program_id(1)

    @pl.when(k == 0)
    def _():
        o_ref[...] = x_ref[...]

    o_ref[...] += _dot(g_ref[...], wd_ref[...])

    if final_norm:
        @pl.when(k == pl.num_programs(1) - 1)
        def _():
            x = o_ref[...]
            y = x * lax.rsqrt(jnp.mean(x * x, axis=-1, keepdims=True) + NORM_EPS)
            o_ref[...] = y * fw_ref[...]


def ffn_down(x, gated, w_down, layer, final_w, *, final_norm, tm=512, tk=2816):
    T, D = x.shape
    nk = D_FF // tk
    return pl.pallas_call(
        functools.partial(_ffn_down_kernel, final_norm=final_norm),
        grid=(T // tm, nk),
        in_specs=[pl.BlockSpec((tm, D), lambda i, k: (i, 0)),
                  pl.BlockSpec((tm, tk), lambda i, k: (i, k)),
                  pl.BlockSpec((None, tk, D), lambda i, k: (layer, k, 0)),
                  pl.BlockSpec((1, D), lambda i, k: (0, 0))],
        out_specs=pl.BlockSpec((tm, D), lambda i, k: (i, 0)),
        out_shape=jax.ShapeDtypeStruct((T, D), f32),
        compiler_params=pltpu.CompilerParams(
            dimension_semantics=("parallel", "arbitrary"), vmem_limit_bytes=VMEM_LIMIT),
        name="ffn_down",
    )(x, gated, w_down, final_w.reshape(1, D))


def _prepare_layer(l, norm1_w, ret_decay_fwd, ret_decay_bwd, ret_norm_w, na_rpb,
                   lq1, lk1, lq2, lk2, diff_norm_w, norm2_w, ffn_conv_w, ffn_conv_b):
    lam_init = 0.8 - 0.6 * math.exp(-0.3 * l)
    lam = (jnp.exp(jnp.sum(lq1[l].astype(f32) * lk1[l].astype(f32)))
           - jnp.exp(jnp.sum(lq2[l].astype(f32) * lk2[l].astype(f32))) + lam_init)
    return dict(
        norm1_w=norm1_w[l].astype(f32),
        lgf=jax.nn.log_sigmoid(ret_decay_fwd[l].astype(f32)),
        lgb=jax.nn.log_sigmoid(ret_decay_bwd[l].astype(f32)),
        ret_norm_w=ret_norm_w[l].astype(f32),
        na_bias=na_bias_table(na_rpb[l]),
        lam=lam.reshape(1), lam_init=lam_init, diff_norm_w=diff_norm_w[l].astype(f32),
        norm2_w=norm2_w[l].astype(f32), conv_w=ffn_conv_w[l].astype(f32), conv_b=ffn_conv_b[l].astype(f32))


def _prepare_weights(w_in, w_out, ffn_w_up, ffn_w_down):
    n0, d0 = 4 * RET_W, 4 * RET_W + 3 * NA_W
    w_in_perm = jnp.concatenate([w_in[:, :, n0:d0], w_in[:, :, d0:], w_in[:, :, :n0]], axis=2)
    return dict(w_in=w_in_perm.astype(bf16), w_out=w_out.astype(bf16),
                w_up=ffn_w_up.astype(bf16), w_down=ffn_w_down.astype(bf16))


def _trunk(x, layers, weights, final_norm_w, slope_vecs):
    B, S, D = x.shape
    x = x.reshape(B * S, D)
    for l, p in enumerate(layers):
        proj = norm_matmul(x, p["norm1_w"], weights["w_in"], l)
        ret = retention(proj, p["lgf"], p["lgb"], p["ret_norm_w"], B=B, S=S)
        na = neighbourhood_attention(proj, p["na_bias"], B=B, S=S)
        diff = diff_attention(proj, slope_vecs, p["lam"], p["diff_norm_w"], B=B, S=S, lam_init=p["lam_init"])
        x = out_proj(x, ret, na, diff, weights["w_out"], l)
        gated = ffn_gate(x, p["norm2_w"], weights["w_up"], l, p["conv_w"], p["conv_b"], S=S)
        x = ffn_down(x, gated, weights["w_down"], l, final_norm_w.astype(f32),
                     final_norm=(l == len(layers) - 1))
    return x.reshape(B, S, D)


def kernel(x_prompt, x_sample, norm1_w, w_in, ret_decay_fwd, ret_decay_bwd, ret_norm_w, na_rpb, diff_lambda_q1, diff_lambda_k1, diff_lambda_q2, diff_lambda_k2, diff_norm_w, w_out, norm2_w, ffn_w_up, ffn_conv_w, ffn_conv_b, ffn_w_down, final_norm_w):
    layers = [_prepare_layer(l, norm1_w, ret_decay_fwd, ret_decay_bwd, ret_norm_w, na_rpb,
                             diff_lambda_q1, diff_lambda_k1, diff_lambda_q2, diff_lambda_k2,
                             diff_norm_w, norm2_w, ffn_conv_w, ffn_conv_b)
              for l in range(w_in.shape[0])]
    weights = _prepare_weights(w_in, w_out, ffn_w_up, ffn_w_down)
    slopes = 2.0 ** (-8.0 * (jnp.arange(DIFF_HEADS, dtype=f32) + 1.0) / DIFF_HEADS)
    slope_vecs = diff_slope_vectors(slopes)
    y_prompt = _trunk(x_prompt, layers, weights, final_norm_w, slope_vecs)
    y_sample = _trunk(x_sample, layers, weights, final_norm_w, slope_vecs)
    return (y_prompt, y_sample)
```

```python
import functools
import math

import jax
import jax.numpy as jnp
import numpy as np
from jax import lax
from jax.experimental import pallas as pl
from jax.experimental.pallas import tpu as pltpu

D_MODEL = 2048
DEPTH = 2
HEAD_DIM = 64
RET_HEADS = 8
RET_W = RET_HEADS * HEAD_DIM
NA_HEADS = 12
NA_W = NA_HEADS * HEAD_DIM
GRID_W = 64
NA_KH = 8
NA_KW = 16
DIFF_HEADS = 6
DIFF_V_DIM = 2 * HEAD_DIM
DIFF_W = DIFF_HEADS * DIFF_V_DIM
D_IN = 4 * RET_W + 3 * NA_W + 3 * DIFF_W
D_FF = 5632
NORM_EPS = 1e-6
NEG_INF = -1e30

LANES = 128
F32_SUBLANES = 8
MXU_WIDTH = 256
RET_CHUNK = 256
RET_UNROLL = 8
FFN_ROW_PIECES = 2
NA_ROWS_PER_STEP = 4
DIFF_TQ = 256
DIFF_POS_RADIX = 64
DIFF_SLOPE_PARTS = 3
LOG2E = 1.4426950408889634
GELU_C1 = 2.0 * math.sqrt(2.0 / math.pi)
GELU_C3 = 8.0 * math.sqrt(2.0 / math.pi) * 0.044715
VMEM_LIMIT = 56 * 1024 * 1024

NA_COL0 = 0
DIFF_COL0 = 3 * NA_W
RET_COL0 = 3 * NA_W + 3 * DIFF_W

f32 = jnp.float32
bf16 = jnp.bfloat16


def _nt_dot(a, b):
    return lax.dot_general(a, b, (((1,), (1,)), ((), ())), preferred_element_type=f32)


def _tn_dot(a, b):
    return lax.dot_general(a, b, (((0,), (0,)), ((), ())), preferred_element_type=f32)


def _dot(a, b):
    return jnp.dot(a, b, preferred_element_type=f32)


def _norm_matmul_kernel(x_ref, nw_ref, w_ref, o_ref, h_ref):
    @pl.when(pl.program_id(1) == 0)
    def _():
        x = x_ref[...]
        y = x * lax.rsqrt(jnp.mean(x * x, axis=-1, keepdims=True) + NORM_EPS)
        h_ref[...] = (y * nw_ref[...]).astype(bf16)

    o_ref[...] = _dot(h_ref[...], w_ref[...]).astype(o_ref.dtype)


def norm_matmul(x, nw, w, layer, *, tm=512, tn=3328):
    T, D = x.shape
    N = w.shape[2]
    return pl.pallas_call(
        _norm_matmul_kernel,
        grid=(T // tm, N // tn),
        in_specs=[
            pl.BlockSpec((tm, D), lambda i, j: (i, 0)),
            pl.BlockSpec((1, D), lambda i, j: (0, 0)),
            pl.BlockSpec((None, D, tn), lambda i, j: (layer, 0, j)),
        ],
        out_specs=pl.BlockSpec((tm, tn), lambda i, j: (i, j)),
        out_shape=jax.ShapeDtypeStruct((T, N), bf16),
        scratch_shapes=[pltpu.VMEM((tm, D), bf16)],
        compiler_params=pltpu.CompilerParams(
            dimension_semantics=("parallel", "arbitrary"), vmem_limit_bytes=VMEM_LIMIT),
        name="norm_matmul",
    )(x, nw.reshape(1, D), w)


def _retention_kernel(lgf_ref, lgb_ref, q_ref, k_ref, v_ref, g_ref, nw_ref, o_ref, acc_ref, *, S):
    C = RET_CHUNK
    n = S // C
    pair = pl.program_id(1)
    scale = HEAD_DIM ** -0.5

    lane = lax.broadcasted_iota(jnp.int32, (1, LANES), 1)
    lo = lane < HEAD_DIM
    lgf = jnp.where(lo, lgf_ref[2 * pair], lgf_ref[2 * pair + 1])
    lgb = jnp.where(lo, lgb_ref[2 * pair], lgb_ref[2 * pair + 1])

    pos = lax.broadcasted_iota(jnp.int32, (C, 1), 0).astype(f32)
    qdec_f = jnp.exp(lgf * (pos + 1.0))
    kdec_f = jnp.exp(lgf * (C - 1.0 - pos)) * scale
    cdec_f = jnp.exp(lgf * float(C))
    qdec_b = jnp.exp(lgb * (C - pos))
    kdec_b = jnp.exp(lgb * pos) * scale
    cdec_b = jnp.exp(lgb * float(C))

    ri = lax.broadcasted_iota(jnp.int32, (C, C), 0)
    ci = lax.broadcasted_iota(jnp.int32, (C, C), 1)
    delta = (ri - ci).astype(f32)

    def decay_mat(h):
        fwd = jnp.exp(lgf_ref[2 * pair + h] * jnp.maximum(delta, 0.0))
        bwd = jnp.exp(lgb_ref[2 * pair + h] * jnp.maximum(-delta, 0.0))
        return jnp.where(delta >= 0, fwd, bwd) * scale

    dmat0 = decay_mat(0)
    dmat1 = decay_mat(1)

    r2 = lax.broadcasted_iota(jnp.int32, (LANES, LANES), 0) // HEAD_DIM
    c2 = lax.broadcasted_iota(jnp.int32, (LANES, LANES), 1) // HEAD_DIM
    same_head = r2 == c2
    avg = jnp.where(same_head, 1.0 / HEAD_DIM, 0.0).astype(bf16)

    def chunk(ref, c):
        return ref[pl.ds(pl.multiple_of(c * C, C), C), :]

    def fwd_body(c, state):
        qc, kc, vc = chunk(q_ref, c), chunk(k_ref, c), chunk(v_ref, c)
        zero = jnp.zeros_like(qc)
        in0 = _nt_dot(jnp.where(lo, qc, zero), kc)
        in1 = _nt_dot(jnp.where(lo, zero, qc), kc)
        w = jnp.concatenate([(in0 * dmat0).astype(bf16), (in1 * dmat1).astype(bf16)], axis=1)
        vv = jnp.concatenate([jnp.where(lo, vc, zero), jnp.where(lo, zero, vc)], axis=0)
        intra = _dot(w, vv)
        qd = (qc.astype(f32) * qdec_f).astype(bf16)
        cross = _dot(qd, state.astype(bf16))
        acc_ref[pl.ds(pl.multiple_of(c * C, C), C), :] = intra + cross
        kd = (kc.astype(f32) * kdec_f).astype(bf16)
        upd = _tn_dot(kd, vc)
        return state * cdec_f + jnp.where(same_head, upd, 0.0)

    lax.fori_loop(0, n, fwd_body, jnp.zeros((LANES, LANES), f32), unroll=RET_UNROLL)

    def lane_mean(y):
        hi = y.astype(bf16)
        lo_part = (y - hi.astype(f32)).astype(bf16)
        return _dot(hi, avg) + _dot(lo_part, avg)

    def bwd_body(t, state):
        c = n - 1 - t
        qc, kc, vc = chunk(q_ref, c), chunk(k_ref, c), chunk(v_ref, c)
        qd = (qc.astype(f32) * qdec_b).astype(bf16)
        y = chunk(acc_ref, c) + _dot(qd, state.astype(bf16))
        y = y - lane_mean(y)
        y = y * lax.rsqrt(lane_mean(y * y) + NORM_EPS)
        gc = chunk(g_ref, c).astype(f32)
        out = jax.nn.silu(gc) * (y * nw_ref[...])
        o_ref[pl.ds(pl.multiple_of(c * C, C), C), :] = out.astype(o_ref.dtype)
        kd = (kc.astype(f32) * kdec_b).astype(bf16)
        upd = _tn_dot(kd, vc)
        return state * cdec_b + jnp.where(same_head, upd, 0.0)

    lax.fori_loop(0, n, bwd_body, jnp.zeros((LANES, LANES), f32), unroll=RET_UNROLL)


def retention(proj, lgf, lgb, norm_w, *, B, S):
    T = B * S
    blk0 = RET_COL0 // LANES
    npair = RET_HEADS // 2

    def col(seg):
        return lambda b, p: (b, blk0 + seg * npair + p)

    smem = pl.BlockSpec(memory_space=pltpu.SMEM)
    return pl.pallas_call(
        functools.partial(_retention_kernel, S=S),
        grid=(B, npair),
        in_specs=[smem, smem,
                  pl.BlockSpec((S, LANES), col(0)),
                  pl.BlockSpec((S, LANES), col(1)),
                  pl.BlockSpec((S, LANES), col(2)),
                  pl.BlockSpec((S, LANES), col(3)),
                  pl.BlockSpec((1, LANES), lambda b, p: (0, p))],
        out_specs=pl.BlockSpec((S, LANES), lambda b, p: (b, p)),
        out_shape=jax.ShapeDtypeStruct((T, RET_W), bf16),
        scratch_shapes=[pltpu.VMEM((S, LANES), f32)],
        compiler_params=pltpu.CompilerParams(
            dimension_semantics=("parallel", "parallel"), vmem_limit_bytes=VMEM_LIMIT),
        name="retention",
    )(lgf, lgb, proj, proj, proj, proj, norm_w.reshape(1, RET_W))


def _na_window_start(r, rows):
    return jnp.clip(r - NA_KH // 2, 0, rows - NA_KH)


def _na_kernel(q_ref, k_ref, v_ref, bias_ref, o_ref, *, rows):
    g = pl.program_id(1)
    nwin = NA_KH * GRID_W
    lane = lax.broadcasted_iota(jnp.int32, (1, LANES), 1)
    lo = lane < HEAD_DIM
    scale = HEAD_DIM ** -0.5
    npair = NA_HEADS // 2
    for j in range(NA_ROWS_PER_STEP):
        r = g * NA_ROWS_PER_STEP + j
        start = pl.multiple_of(_na_window_start(r, rows) * GRID_W, GRID_W)
        tok = slice(j * GRID_W, (j + 1) * GRID_W)
        parts = []
        for p in range(npair):
            cols = slice(p * LANES, (p + 1) * LANES)
            qp = (q_ref[tok, cols].astype(f32) * scale).astype(bf16)
            kp = k_ref[pl.ds(start, nwin), cols]
            zero = jnp.zeros_like(qp)
            parts.append(_nt_dot(jnp.where(lo, qp, zero), kp))
            parts.append(_nt_dot(jnp.where(lo, zero, qp), kp))
        d0 = NA_KH - 1 - (r - _na_window_start(r, rows))
        bias = jnp.concatenate(
            [jnp.concatenate([bias_ref[h, d0 + w] for w in range(0, NA_KH, 2)], axis=1)
             for h in range(NA_HEADS)], axis=0)
        s = jnp.concatenate(parts, axis=0) + bias
        e = jnp.exp(s - jnp.max(s, axis=-1, keepdims=True))
        inv = 1.0 / jnp.sum(e, axis=-1, keepdims=True)
        eb = e.astype(bf16)
        for p in range(npair):
            cols = slice(p * LANES, (p + 1) * LANES)
            vp = v_ref[pl.ds(start, nwin), cols]
            r0, r1, r2 = 2 * p * GRID_W, (2 * p + 1) * GRID_W, (2 * p + 2) * GRID_W
            even = _dot(eb[r0:r1], vp) * inv[r0:r1]
            odd = _dot(eb[r1:r2], vp) * inv[r1:r2]
            o_ref[tok, cols] = jnp.where(lo, even, odd).astype(o_ref.dtype)


def neighbourhood_attention(proj, bias_table, *, B, S):
    T = B * S
    rows = S // GRID_W
    nwin = NA_KH * GRID_W
    R = NA_ROWS_PER_STEP
    steps = rows // R
    assert rows % R == 0 and rows >= NA_KH

    return pl.pallas_call(
        functools.partial(_na_kernel, rows=rows),
        grid=(B, steps),
        in_specs=[pl.BlockSpec((R * GRID_W, NA_W), lambda b, g: (b * steps + g, 0)),
                  pl.BlockSpec((S, NA_W), lambda b, g: (b, 1)),
                  pl.BlockSpec((S, NA_W), lambda b, g: (b, 2)),
                  pl.BlockSpec(bias_table.shape, lambda b, g: (0, 0, 0, 0))],
        out_specs=pl.BlockSpec((R * GRID_W, NA_W), lambda b, g: (b * steps + g, 0)),
        out_shape=jax.ShapeDtypeStruct((T, NA_W), bf16),
        compiler_params=pltpu.CompilerParams(
            dimension_semantics=("parallel", "arbitrary"), vmem_limit_bytes=VMEM_LIMIT),
        name="neighbourhood_attention",
    )(proj, proj, proj, bias_table)


def na_bias_table(rpb):
    qcol = np.arange(GRID_W)[:, None]
    kcol = np.arange(GRID_W)[None, :]
    cs = np.clip(qcol - NA_KW // 2, 0, GRID_W - NA_KW)
    in_win = (kcol >= cs) & (kcol < cs + NA_KW)
    dc = np.clip(kcol - qcol + NA_KW - 1, 0, 2 * NA_KW - 2)
    blocks = jnp.where(jnp.asarray(in_win), rpb.astype(f32)[:, :, dc], NEG_INF)
    return jnp.concatenate([blocks[:, :-1], blocks[:, 1:]], axis=-1)


def _diff_position_features(pos, lane_in_half, first):
    f = lane_in_half - first
    lo = jnp.bitwise_and(pos, DIFF_POS_RADIX - 1)
    hi = pos - lo
    val = jnp.where(jnp.bitwise_and(f, 1) == 0, hi, lo)
    return jnp.where((f >= 0) & (f < 2 * DIFF_SLOPE_PARTS), val, 0).astype(f32)


def _diff_kernel(lam_ref, q_ref, qnext_ref, k_ref, v_ref, cvec_ref, nw_ref, o_ref,
                 kaug_ref, vaug_ref, qv_ref, s_even_ref, s_odd_ref, m_even_ref, m_odd_ref, *, S, out_scale):
    tq = DIFF_TQ
    nkt = S // tq
    qi = pl.program_id(2)
    lane = lax.broadcasted_iota(jnp.int32, (1, LANES), 1)
    lo = lane < HEAD_DIM
    lane_in_half = jnp.bitwise_and(lane, HEAD_DIM - 1)
    cq = cvec_ref[0:1, :]
    ck = cvec_ref[1:2, :]

    def key_rows(tile, r):
        t = jnp.bitwise_and(tile + r, nkt - 1)
        return pl.ds(pl.multiple_of(t * tq, tq), tq)

    def halves_max(x):
        return jnp.maximum(x[:, :LANES], x[:, LANES:])

    def pipeline(q_blk_ref, nxt_tile, nxt_s_ref, nxt_m_ref, cur_tile, cur_s_ref, cur_m_ref, out_ref):
        if q_blk_ref is not None:
            qs = (q_blk_ref[...].astype(f32) * (HEAD_DIM ** -0.5 * LOG2E)).astype(bf16)
            qpos = nxt_tile * tq + lax.broadcasted_iota(jnp.int32, (tq, 1), 0)
            fq = _diff_position_features(qpos, lane_in_half, 0) + cq
            zero = jnp.zeros_like(qs)
            for v, f in enumerate((fq.astype(bf16), zero, (-fq).astype(bf16))):
                qv_ref[v] = jnp.where(lo, qs, f)
                qv_ref[3 + v] = jnp.where(lo, f, qs)
            d = (lax.broadcasted_iota(jnp.int32, (tq, tq), 0)
                 - lax.broadcasted_iota(jnp.int32, (tq, tq), 1)).astype(f32)
            diag_bias = -cvec_ref[2:3, 0:1] * jnp.abs(d)

        if cur_s_ref is not None:
            m_cur = [jnp.max(cur_m_ref[c], axis=1, keepdims=True) for c in range(2)]
            acc = [None, None]
        m_acc = [None, None]
        for r in range(nkt):
            if cur_s_ref is not None:
                v_ones = vaug_ref[key_rows(cur_tile, r), :]
                for c in range(2):
                    e = jnp.exp2(cur_s_ref[c, r] - m_cur[c]).astype(bf16)
                    pv = _dot(e, v_ones)
                    acc[c] = pv if acc[c] is None else acc[c] + pv
            if q_blk_ref is None:
                continue
            rows = key_rows(nxt_tile, r)
            variant = 1 if r == 0 else jnp.where(nxt_tile + r >= nkt, 0, 2)
            for c in range(2):
                s = _nt_dot(qv_ref[3 * c + variant], kaug_ref[c, rows, :])
                if r == 0:
                    s = s + diag_bias
                nxt_s_ref[c, r] = s
                m_acc[c] = halves_max(s) if r == 0 else jnp.maximum(m_acc[c], halves_max(s))
        if q_blk_ref is not None:
            for c in range(2):
                nxt_m_ref[c] = m_acc[c]
        if cur_s_ref is None:
            return

        r1 = 1.0 / acc[0][:, DIFF_V_DIM:DIFF_V_DIM + 1]
        r2 = lam_ref[0] / acc[1][:, DIFF_V_DIM:DIFF_V_DIM + 1]
        out = acc[0][:, :DIFF_V_DIM] * r1 - acc[1][:, :DIFF_V_DIM] * r2
        y = out * lax.rsqrt(jnp.mean(out * out, axis=-1, keepdims=True) + NORM_EPS)
        out_ref[...] = (y * nw_ref[...] * out_scale).astype(out_ref.dtype)

    @pl.when(qi == 0)
    def _():
        kpos = lax.broadcasted_iota(jnp.int32, (S, 1), 0)
        fk = (_diff_position_features(kpos, lane_in_half, 2 * DIFF_SLOPE_PARTS) + ck).astype(bf16)
        k = k_ref[...]
        kaug_ref[0] = jnp.where(lo, k, fk)
        kaug_ref[1] = jnp.where(lo, fk, k)
        vaug_ref[:, 0:DIFF_V_DIM] = v_ref[...]
        vaug_ref[:, DIFF_V_DIM:] = jnp.broadcast_to(jnp.where(lane == 0, 1.0, 0.0).astype(bf16), (S, LANES))
        pipeline(q_ref.at[0:tq], 0, s_even_ref, m_even_ref, None, None, None, None)

    even = 2 * qi
    last = nkt // 2 - 1
    pipeline(q_ref.at[tq:2 * tq], even + 1, s_odd_ref, m_odd_ref, even, s_even_ref, m_even_ref, o_ref.at[0:tq])

    @pl.when(qi < last)
    def _():
        pipeline(qnext_ref, even + 2, s_even_ref, m_even_ref, even + 1, s_odd_ref, m_odd_ref,
                 o_ref.at[tq:2 * tq])

    @pl.when(qi == last)
    def _():
        pipeline(None, None, None, None, even + 1, s_odd_ref, m_odd_ref, o_ref.at[tq:2 * tq])


def diff_slope_vectors(slopes):
    c = slopes.astype(f32) * LOG2E
    parts = []
    rest = c
    for _ in range(DIFF_SLOPE_PARTS):
        p = rest.astype(bf16).astype(f32)
        parts.append(p)
        rest = rest - p
    parts = jnp.stack(parts, axis=1)
    lane_in_half = np.arange(LANES) % HEAD_DIM
    n = 2 * DIFF_SLOPE_PARTS
    q_sel = np.where((lane_in_half >= n) & (lane_in_half < 2 * n), (lane_in_half - n) // 2, -1)
    k_sel = np.where(lane_in_half < n, lane_in_half // 2, -1)
    zero = jnp.zeros((slopes.shape[0], 1), f32)
    padded = jnp.concatenate([parts, zero], axis=1)
    cq = padded[:, q_sel]
    ck = -padded[:, k_sel]
    return jnp.stack([cq, ck, jnp.broadcast_to(c[:, None], cq.shape)], axis=1)


def diff_attention(proj, cvec, lam, norm_w, *, B, S, lam_init):
    T = B * S
    tq = DIFF_TQ
    nq = S // tq
    assert S <= DIFF_POS_RADIX * DIFF_POS_RADIX and nq & (nq - 1) == 0 and nq % 2 == 0
    steps = nq // 2
    blk0 = DIFF_COL0 // LANES
    smem = pl.BlockSpec(memory_space=pltpu.SMEM)
    return pl.pallas_call(
        functools.partial(_diff_kernel, S=S, out_scale=1.0 - lam_init),
        grid=(B, DIFF_HEADS, steps),
        in_specs=[smem,
                  pl.BlockSpec((2 * tq, LANES), lambda b, h, i: (b * steps + i, blk0 + h)),
                  pl.BlockSpec((tq, LANES), lambda b, h, i: (b * nq + jnp.minimum(2 * i + 2, nq - 1), blk0 + h)),
                  pl.BlockSpec((S, LANES), lambda b, h, i: (b, blk0 + DIFF_HEADS + h)),
                  pl.BlockSpec((S, LANES), lambda b, h, i: (b, blk0 + 2 * DIFF_HEADS + h)),
                  pl.BlockSpec((None, 3, LANES), lambda b, h, i: (h, 0, 0)),
                  pl.BlockSpec((1, LANES), lambda b, h, i: (0, h))],
        out_specs=pl.BlockSpec((2 * tq, LANES), lambda b, h, i: (b * steps + i, h)),
        out_shape=jax.ShapeDtypeStruct((T, DIFF_W), bf16),
        scratch_shapes=[pltpu.VMEM((2, S, LANES), bf16),
                        pltpu.VMEM((S, 2 * LANES), bf16),
                        pltpu.VMEM((6, tq, LANES), bf16),
                        pltpu.VMEM((2, nq, tq, tq), f32),
                        pltpu.VMEM((2, nq, tq, tq), f32),
                        pltpu.VMEM((2, tq, LANES), f32),
                        pltpu.VMEM((2, tq, LANES), f32)],
        compiler_params=pltpu.CompilerParams(
            dimension_semantics=("parallel", "parallel", "arbitrary"), vmem_limit_bytes=VMEM_LIMIT),
        name="diff_attention",
    )(lam, proj, proj, proj, proj, cvec, norm_w.reshape(1, DIFF_W))


def _out_proj_kernel(x_ref, r_ref, n_ref, d_ref, w_ref, o_ref):
    acc = _dot(r_ref[...], w_ref[0:RET_W, :])
    acc += _dot(n_ref[...], w_ref[RET_W:RET_W + NA_W, :])
    acc += _dot(d_ref[...], w_ref[RET_W + NA_W:, :])
    o_ref[...] = x_ref[...] + acc


def out_proj(x, ret, na, diff, w_out, layer, *, tm=512):
    T, D = x.shape
    row = lambda i: (i, 0)
    return pl.pallas_call(
        _out_proj_kernel,
        grid=(T // tm,),
        in_specs=[pl.BlockSpec((tm, D), row),
                  pl.BlockSpec((tm, RET_W), row),
                  pl.BlockSpec((tm, NA_W), row),
                  pl.BlockSpec((tm, DIFF_W), row),
                  pl.BlockSpec((None, D, D), lambda i: (layer, 0, 0))],
        out_specs=pl.BlockSpec((tm, D), row),
        out_shape=jax.ShapeDtypeStruct((T, D), f32),
        compiler_params=pltpu.CompilerParams(
            dimension_semantics=("parallel",), vmem_limit_bytes=VMEM_LIMIT),
        name="out_proj",
    )(x, ret, na, diff, w_out)


def _rms(x):
    return x * lax.rsqrt(jnp.mean(x * x, axis=-1, keepdims=True) + NORM_EPS)


def _ffn_gate_kernel(x_ref, xp_ref, xn_ref, nw_ref, wa_ref, wg_ref, hw_ref, o_ref, h_ref, hh_ref,
                     *, S, tm):
    i = pl.program_id(0)

    @pl.when(pl.program_id(1) == 0)
    def _():
        nw = nw_ref[...]
        h_ref[...] = (_rms(x_ref[...]) * nw).astype(bf16)
        hh_ref[0:F32_SUBLANES, :] = (_rms(xp_ref[...]) * nw).astype(bf16)
        hh_ref[F32_SUBLANES:, :] = (_rms(xn_ref[...]) * nw).astype(bf16)

    hh = hh_ref[...]
    t0 = i * tm
    at_seq_start = (t0 % S) == 0
    at_seq_end = ((t0 + tm) % S) == 0
    pm = tm // FFN_ROW_PIECES
    edge = 2 * F32_SUBLANES
    span = edge + F32_SUBLANES
    row = lax.broadcasted_iota(jnp.int32, (pm, 1), 0)
    seam_row = lax.broadcasted_iota(jnp.int32, (span, 1), 0)

    def gate(g_up, g_mid, g_dn, a, cols):
        half = g_up * hw_ref[0:1, cols] + g_mid * hw_ref[1:2, cols] + g_dn * hw_ref[2:3, cols] + hw_ref[3:4, cols]
        u = half * (GELU_C1 + GELU_C3 * (half * half))
        ah = a * half
        return ah + ah * jnp.tanh(u)

    for c0 in range(0, o_ref.shape[1], MXU_WIDTH):
        cols = slice(c0, c0 + MXU_WIDTH)
        wg = wg_ref[:, cols]
        wa = wa_ref[:, cols]
        g_halo = _dot(hh, wg)
        above = jnp.where(at_seq_start, 0.0, g_halo[F32_SUBLANES - 1:F32_SUBLANES, :])
        below_tile = jnp.where(at_seq_end, 0.0, g_halo[F32_SUBLANES:F32_SUBLANES + 1, :])
        held = None
        for p in range(FFN_ROW_PIECES):
            r0 = p * pm
            g = _dot(h_ref[r0:r0 + pm, :], wg)
            a = _dot(h_ref[r0:r0 + pm, :], wa)
            if held is not None:
                g_seam, a_seam, seam_at = held
                dn = jnp.where(seam_row == span - 1, g[0:1, :], pltpu.roll(g_seam, span - 1, 0))
                seam = gate(pltpu.roll(g_seam, 1, 0), g_seam, dn, a_seam, cols)
                o_ref[seam_at - edge:seam_at, cols] = seam[span - edge:span].astype(o_ref.dtype)
            up = jnp.where(row == 0, above, pltpu.roll(g, 1, 0))
            dn = pltpu.roll(g, pm - 1, 0)
            if p == FFN_ROW_PIECES - 1:
                dn = jnp.where(row == pm - 1, below_tile, dn)
            o_ref[r0:r0 + pm, cols] = gate(up, g, dn, a, cols).astype(o_ref.dtype)
            above = g[pm - 1:pm, :]
            held = (g[pm - span:pm, :], a[pm - span:pm, :], r0 + pm)


def ffn_gate(x, nw, w_up, layer, conv_w, conv_b, *, S, tm=1024, tf=512):
    T, D = x.shape
    half_conv = 0.5 * jnp.concatenate([conv_w, conv_b.reshape(1, D_FF)], axis=0)
    nf = D_FF // tf
    rb = tm // F32_SUBLANES
    last_rb = T // F32_SUBLANES - 1
    assert S % tm == 0
    return pl.pallas_call(
        functools.partial(_ffn_gate_kernel, S=S, tm=tm),
        grid=(T // tm, nf),
        in_specs=[pl.BlockSpec((tm, D), lambda i, j: (i, 0)),
                  pl.BlockSpec((F32_SUBLANES, D), lambda i, j: (jnp.maximum(i * rb - 1, 0), 0)),
                  pl.BlockSpec((F32_SUBLANES, D), lambda i, j: (jnp.minimum((i + 1) * rb, last_rb), 0)),
                  pl.BlockSpec((1, D), lambda i, j: (0, 0)),
                  pl.BlockSpec((None, D, tf), lambda i, j: (layer, 0, j)),
                  pl.BlockSpec((None, D, tf), lambda i, j: (layer, 0, nf + j)),
                  pl.BlockSpec((4, tf), lambda i, j: (0, j))],
        out_specs=pl.BlockSpec((tm, tf), lambda i, j: (i, j)),
        out_shape=jax.ShapeDtypeStruct((T, D_FF), bf16),
        scratch_shapes=[pltpu.VMEM((tm, D), bf16), pltpu.VMEM((2 * F32_SUBLANES, D), bf16)],
        compiler_params=pltpu.CompilerParams(
            dimension_semantics=("parallel", "arbitrary"), vmem_limit_bytes=VMEM_LIMIT),
        name="ffn_gate",
    )(x, x, x, nw.reshape(1, D), w_up, w_up, half_conv)


def _ffn_down_kernel(x_ref, g_ref, wd_ref, fw_ref, o_ref, *, final_norm):
    k = pl.program_id(1)

    @pl.when(k == 0)
    def _():
        o_ref[...] = x_ref[...]

    o_ref[...] += _dot(g_ref[...], wd_ref[...])

    if final_norm:
        @pl.when(k == pl.num_programs(1) - 1)
        def _():
            x = o_ref[...]
            y = x * lax.rsqrt(jnp.mean(x * x, axis=-1, keepdims=True) + NORM_EPS)
            o_ref[...] = y * fw_ref[...]


def ffn_down(x, gated, w_down, layer, final_w, *, final_norm, tm=512, tk=2816):
    T, D = x.shape
    nk = D_FF // tk
    return pl.pallas_call(
        functools.partial(_ffn_down_kernel, final_norm=final_norm),
        grid=(T // tm, nk),
        in_specs=[pl.BlockSpec((tm, D), lambda i, k: (i, 0)),
                  pl.BlockSpec((tm, tk), lambda i, k: (i, k)),
                  pl.BlockSpec((None, tk, D), lambda i, k: (layer, k, 0)),
                  pl.BlockSpec((1, D), lambda i, k: (0, 0))],
        out_specs=pl.BlockSpec((tm, D), lambda i, k: (i, 0)),
        out_shape=jax.ShapeDtypeStruct((T, D), f32),
        compiler_params=pltpu.CompilerParams(
            dimension_semantics=("parallel", "arbitrary"), vmem_limit_bytes=VMEM_LIMIT),
        name="ffn_down",
    )(x, gated, w_down, final_w.reshape(1, D))


def _prepare_layer(l, norm1_w, ret_decay_fwd, ret_decay_bwd, ret_norm_w, na_rpb,
                   lq1, lk1, lq2, lk2, diff_norm_w, norm2_w, ffn_conv_w, ffn_conv_b):
    lam_init = 0.8 - 0.6 * math.exp(-0.3 * l)
    lam = (jnp.exp(jnp.sum(lq1[l].astype(f32) * lk1[l].astype(f32)))
           - jnp.exp(jnp.sum(lq2[l].astype(f32) * lk2[l].astype(f32))) + lam_init)
    return dict(
        norm1_w=norm1_w[l].astype(f32),
        lgf=jax.nn.log_sigmoid(ret_decay_fwd[l].astype(f32)),
        lgb=jax.nn.log_sigmoid(ret_decay_bwd[l].astype(f32)),
        ret_norm_w=ret_norm_w[l].astype(f32),
        na_bias=na_bias_table(na_rpb[l]),
        lam=lam.reshape(1), lam_init=lam_init, diff_norm_w=diff_norm_w[l].astype(f32),
        norm2_w=norm2_w[l].astype(f32), conv_w=ffn_conv_w[l].astype(f32), conv_b=ffn_conv_b[l].astype(f32))


def _prepare_weights(w_in, w_out, ffn_w_up, ffn_w_down):
    n0, d0 = 4 * RET_W, 4 * RET_W + 3 * NA_W
    w_in_perm = jnp.concatenate([w_in[:, :, n0:d0], w_in[:, :, d0:], w_in[:, :, :n0]], axis=2)
    return dict(w_in=w_in_perm.astype(bf16), w_out=w_out.astype(bf16),
                w_up=ffn_w_up.astype(bf16), w_down=ffn_w_down.astype(bf16))


def _trunk(x, layers, weights, final_norm_w, slope_vecs):
    B, S, D = x.shape
    x = x.reshape(B * S, D)
    for l, p in enumerate(layers):
        proj = norm_matmul(x, p["norm1_w"], weights["w_in"], l)
        ret = retention(proj, p["lgf"], p["lgb"], p["ret_norm_w"], B=B, S=S)
        na = neighbourhood_attention(proj, p["na_bias"], B=B, S=S)
        diff = diff_attention(proj, slope_vecs, p["lam"], p["diff_norm_w"], B=B, S=S, lam_init=p["lam_init"])
        x = out_proj(x, ret, na, diff, weights["w_out"], l)
        gated = ffn_gate(x, p["norm2_w"], weights["w_up"], l, p["conv_w"], p["conv_b"], S=S)
        x = ffn_down(x, gated, weights["w_down"], l, final_norm_w.astype(f32),
                     final_norm=(l == len(layers) - 1))
    return x.reshape(B, S, D)


def kernel(x_prompt, x_sample, norm1_w, w_in, ret_decay_fwd, ret_decay_bwd, ret_norm_w, na_rpb, diff_lambda_q1, diff_lambda_k1, diff_lambda_q2, diff_lambda_k2, diff_norm_w, w_out, norm2_w, ffn_w_up, ffn_conv_w, ffn_conv_b, ffn_w_down, final_norm_w):
    layers = [_prepare_layer(l, norm1_w, ret_decay_fwd, ret_decay_bwd, ret_norm_w, na_rpb,
                             diff_lambda_q1, diff_lambda_k1, diff_lambda_q2, diff_lambda_k2,
                             diff_norm_w, norm2_w, ffn_conv_w, ffn_conv_b)
              for l in range(w_in.shape[0])]
    weights = _prepare_weights(w_in, w_out, ffn_w_up, ffn_w_down)
    slopes = 2.0 ** (-8.0 * (jnp.arange(DIFF_HEADS, dtype=f32) + 1.0) / DIFF_HEADS)
    slope_vecs = diff_slope_vectors(slopes)
    y_prompt = _trunk(x_prompt, layers, weights, final_norm_w, slope_vecs)
    y_sample = _trunk(x_sample, layers, weights, final_norm_w, slope_vecs)
    return (y_prompt, y_sample)
```

```python
import functools
import math

import jax
import jax.numpy as jnp
import numpy as np
from jax import lax
from jax.experimental import pallas as pl
from jax.experimental.pallas import tpu as pltpu

D_MODEL = 2048
DEPTH = 2
HEAD_DIM = 64
RET_HEADS = 8
RET_W = RET_HEADS * HEAD_DIM
NA_HEADS = 12
NA_W = NA_HEADS * HEAD_DIM
GRID_W = 64
NA_KH = 8
NA_KW = 16
DIFF_HEADS = 6
DIFF_V_DIM = 2 * HEAD_DIM
DIFF_W = DIFF_HEADS * DIFF_V_DIM
D_IN = 4 * RET_W + 3 * NA_W + 3 * DIFF_W
D_FF = 5632
NORM_EPS = 1e-6
NEG_INF = -1e30

LANES = 128
F32_SUBLANES = 8
MXU_WIDTH = 256
RET_CHUNK = 256
RET_UNROLL = 8
FFN_ROW_PIECES = 2
NA_ROWS_PER_STEP = 4
DIFF_TQ = 256
DIFF_POS_RADIX = 64
DIFF_SLOPE_PARTS = 3
LOG2E = 1.4426950408889634
GELU_C1 = 2.0 * math.sqrt(2.0 / math.pi)
GELU_C3 = 8.0 * math.sqrt(2.0 / math.pi) * 0.044715
VMEM_LIMIT = 56 * 1024 * 1024

NA_COL0 = 0
DIFF_COL0 = 3 * NA_W
RET_COL0 = 3 * NA_W + 3 * DIFF_W

f32 = jnp.float32
bf16 = jnp.bfloat16


def _nt_dot(a, b):
    return lax.dot_general(a, b, (((1,), (1,)), ((), ())), preferred_element_type=f32)


def _tn_dot(a, b):
    return lax.dot_general(a, b, (((0,), (0,)), ((), ())), preferred_element_type=f32)


def _dot(a, b):
    return jnp.dot(a, b, preferred_element_type=f32)


def _norm_matmul_kernel(x_ref, nw_ref, w_ref, o_ref, h_ref):
    @pl.when(pl.program_id(1) == 0)
    def _():
        x = x_ref[...]
        y = x * lax.rsqrt(jnp.mean(x * x, axis=-1, keepdims=True) + NORM_EPS)
        h_ref[...] = (y * nw_ref[...]).astype(bf16)

    o_ref[...] = _dot(h_ref[...], w_ref[...]).astype(o_ref.dtype)


def norm_matmul(x, nw, w, layer, *, tm=512, tn=3328):
    T, D = x.shape
    N = w.shape[2]
    return pl.pallas_call(
        _norm_matmul_kernel,
        grid=(T // tm, N // tn),
        in_specs=[
            pl.BlockSpec((tm, D), lambda i, j: (i, 0)),
            pl.BlockSpec((1, D), lambda i, j: (0, 0)),
            pl.BlockSpec((None, D, tn), lambda i, j: (layer, 0, j)),
        ],
        out_specs=pl.BlockSpec((tm, tn), lambda i, j: (i, j)),
        out_shape=jax.ShapeDtypeStruct((T, N), bf16),
        scratch_shapes=[pltpu.VMEM((tm, D), bf16)],
        compiler_params=pltpu.CompilerParams(
            dimension_semantics=("parallel", "arbitrary"), vmem_limit_bytes=VMEM_LIMIT),
        name="norm_matmul",
    )(x, nw.reshape(1, D), w)


def _retention_kernel(lgf_ref, lgb_ref, q_ref, k_ref, v_ref, g_ref, nw_ref, o_ref, acc_ref, *, S):
    C = RET_CHUNK
    n = S // C
    pair = pl.program_id(1)
    scale = HEAD_DIM ** -0.5

    lane = lax.broadcasted_iota(jnp.int32, (1, LANES), 1)
    lo = lane < HEAD_DIM
    lgf = jnp.where(lo, lgf_ref[2 * pair], lgf_ref[2 * pair + 1])
    lgb = jnp.where(lo, lgb_ref[2 * pair], lgb_ref[2 * pair + 1])

    pos = lax.broadcasted_iota(jnp.int32, (C, 1), 0).astype(f32)
    qdec_f = jnp.exp(lgf * (pos + 1.0))
    kdec_f = jnp.exp(lgf * (C - 1.0 - pos)) * scale
    cdec_f = jnp.exp(lgf * float(C))
    qdec_b = jnp.exp(lgb * (C - pos))
    kdec_b = jnp.exp(lgb * pos) * scale
    cdec_b = jnp.exp(lgb * float(C))

    ri = lax.broadcasted_iota(jnp.int32, (C, C), 0)
    ci = lax.broadcasted_iota(jnp.int32, (C, C), 1)
    delta = (ri - ci).astype(f32)

    def decay_mat(h):
        fwd = jnp.exp(lgf_ref[2 * pair + h] * jnp.maximum(delta, 0.0))
        bwd = jnp.exp(lgb_ref[2 * pair + h] * jnp.maximum(-delta, 0.0))
        return jnp.where(delta >= 0, fwd, bwd) * scale

    dmat0 = decay_mat(0)
    dmat1 = decay_mat(1)

    r2 = lax.broadcasted_iota(jnp.int32, (LANES, LANES), 0) // HEAD_DIM
    c2 = lax.broadcasted_iota(jnp.int32, (LANES, LANES), 1) // HEAD_DIM
    same_head = r2 == c2
    avg = jnp.where(same_head, 1.0 / HEAD_DIM, 0.0).astype(bf16)

    def chunk(ref, c):
        return ref[pl.ds(pl.multiple_of(c * C, C), C), :]

    def fwd_body(c, state):
        qc, kc, vc = chunk(q_ref, c), chunk(k_ref, c), chunk(v_ref, c)
        zero = jnp.zeros_like(qc)
        in0 = _nt_dot(jnp.where(lo, qc, zero), kc)
        in1 = _nt_dot(jnp.where(lo, zero, qc), kc)
        w = jnp.concatenate([(in0 * dmat0).astype(bf16), (in1 * dmat1).astype(bf16)], axis=1)
        vv = jnp.concatenate([jnp.where(lo, vc, zero), jnp.where(lo, zero, vc)], axis=0)
        intra = _dot(w, vv)
        qd = (qc.astype(f32) * qdec_f).astype(bf16)
        cross = _dot(qd, state.astype(bf16))
        acc_ref[pl.ds(pl.multiple_of(c * C, C), C), :] = intra + cross
        kd = (kc.astype(f32) * kdec_f).astype(bf16)
        upd = _tn_dot(kd, vc)
        return state * cdec_f + jnp.where(same_head, upd, 0.0)

    lax.fori_loop(0, n, fwd_body, jnp.zeros((LANES, LANES), f32), unroll=RET_UNROLL)

    def lane_mean(y):
        hi = y.astype(bf16)
        lo_part = (y - hi.astype(f32)).astype(bf16)
        return _dot(hi, avg) + _dot(lo_part, avg)

    def bwd_body(t, state):
        c = n - 1 - t
        qc, kc, vc = chunk(q_ref, c), chunk(k_ref, c), chunk(v_ref, c)
        qd = (qc.astype(f32) * qdec_b).astype(bf16)
        y = chunk(acc_ref, c) + _dot(qd, state.astype(bf16))
        y = y - lane_mean(y)
        y = y * lax.rsqrt(lane_mean(y * y) + NORM_EPS)
        gc = chunk(g_ref, c).astype(f32)
        out = jax.nn.silu(gc) * (y * nw_ref[...])
        o_ref[pl.ds(pl.multiple_of(c * C, C), C), :] = out.astype(o_ref.dtype)
        kd = (kc.astype(f32) * kdec_b).astype(bf16)
        upd = _tn_dot(kd, vc)
        return state * cdec_b + jnp.where(same_head, upd, 0.0)

    lax.fori_loop(0, n, bwd_body, jnp.zeros((LANES, LANES), f32), unroll=RET_UNROLL)


def retention(proj, lgf, lgb, norm_w, *, B, S):
    T = B * S
    blk0 = RET_COL0 // LANES
    npair = RET_HEADS // 2

    def col(seg):
        return lambda b, p: (b, blk0 + seg * npair + p)

    smem = pl.BlockSpec(memory_space=pltpu.SMEM)
    return pl.pallas_call(
        functools.partial(_retention_kernel, S=S),
        grid=(B, npair),
        in_specs=[smem, smem,
                  pl.BlockSpec((S, LANES), col(0)),
                  pl.BlockSpec((S, LANES), col(1)),
                  pl.BlockSpec((S, LANES), col(2)),
                  pl.BlockSpec((S, LANES), col(3)),
                  pl.BlockSpec((1, LANES), lambda b, p: (0, p))],
        out_specs=pl.BlockSpec((S, LANES), lambda b, p: (b, p)),
        out_shape=jax.ShapeDtypeStruct((T, RET_W), bf16),
        scratch_shapes=[pltpu.VMEM((S, LANES), f32)],
        compiler_params=pltpu.CompilerParams(
            dimension_semantics=("parallel", "parallel"), vmem_limit_bytes=VMEM_LIMIT),
        name="retention",
    )(lgf, lgb, proj, proj, proj, proj, norm_w.reshape(1, RET_W))


def _na_window_start(r, rows):
    return jnp.clip(r - NA_KH // 2, 0, rows - NA_KH)


def _na_kernel(q_ref, k_ref, v_ref, bias_ref, o_ref, *, rows):
    g = pl.program_id(1)
    nwin = NA_KH * GRID_W
    lane = lax.broadcasted_iota(jnp.int32, (1, LANES), 1)
    lo = lane < HEAD_DIM
    scale = HEAD_DIM ** -0.5
    npair = NA_HEADS // 2
    for j in range(NA_ROWS_PER_STEP):
        r = g * NA_ROWS_PER_STEP + j
        start = pl.multiple_of(_na_window_start(r, rows) * GRID_W, GRID_W)
        tok = slice(j * GRID_W, (j + 1) * GRID_W)
        parts = []
        for p in range(npair):
            cols = slice(p * LANES, (p + 1) * LANES)
            qp = (q_ref[tok, cols].astype(f32) * scale).astype(bf16)
            kp = k_ref[pl.ds(start, nwin), cols]
            zero = jnp.zeros_like(qp)
            parts.append(_nt_dot(jnp.where(lo, qp, zero), kp))
            parts.append(_nt_dot(jnp.where(lo, zero, qp), kp))
        d0 = NA_KH - 1 - (r - _na_window_start(r, rows))
        bias = jnp.concatenate(
            [jnp.concatenate([bias_ref[h, d0 + w] for w in range(0, NA_KH, 2)], axis=1)
             for h in range(NA_HEADS)], axis=0)
        s = jnp.concatenate(parts, axis=0) + bias
        e = jnp.exp(s - jnp.max(s, axis=-1, keepdims=True))
        inv = 1.0 / jnp.sum(e, axis=-1, keepdims=True)
        eb = e.astype(bf16)
        for p in range(npair):
            cols = slice(p * LANES, (p + 1) * LANES)
            vp = v_ref[pl.ds(start, nwin), cols]
            r0, r1, r2 = 2 * p * GRID_W, (2 * p + 1) * GRID_W, (2 * p + 2) * GRID_W
            even = _dot(eb[r0:r1], vp) * inv[r0:r1]
            odd = _dot(eb[r1:r2], vp) * inv[r1:r2]
            o_ref[tok, cols] = jnp.where(lo, even, odd).astype(o_ref.dtype)


def neighbourhood_attention(proj, bias_table, *, B, S):
    T = B * S
    rows = S // GRID_W
    nwin = NA_KH * GRID_W
    R = NA_ROWS_PER_STEP
    steps = rows // R
    assert rows % R == 0 and rows >= NA_KH

    return pl.pallas_call(
        functools.partial(_na_kernel, rows=rows),
        grid=(B, steps),
        in_specs=[pl.BlockSpec((R * GRID_W, NA_W), lambda b, g: (b * steps + g, 0)),
                  pl.BlockSpec((S, NA_W), lambda b, g: (b, 1)),
                  pl.BlockSpec((S, NA_W), lambda b, g: (b, 2)),
                  pl.BlockSpec(bias_table.shape, lambda b, g: (0, 0, 0, 0))],
        out_specs=pl.BlockSpec((R * GRID_W, NA_W), lambda b, g: (b * steps + g, 0)),
        out_shape=jax.ShapeDtypeStruct((T, NA_W), bf16),
        compiler_params=pltpu.CompilerParams(
            dimension_semantics=("parallel", "arbitrary"), vmem_limit_bytes=VMEM_LIMIT),
        name="neighbourhood_attention",
    )(proj, proj, proj, bias_table)


def na_bias_table(rpb):
    qcol = np.arange(GRID_W)[:, None]
    kcol = np.arange(GRID_W)[None, :]
    cs = np.clip(qcol - NA_KW // 2, 0, GRID_W - NA_KW)
    in_win = (kcol >= cs) & (kcol < cs + NA_KW)
    dc = np.clip(kcol - qcol + NA_KW - 1, 0, 2 * NA_KW - 2)
    onehot = ((dc[None] == np.arange(2 * NA_KW - 1)[:, None, None]) & in_win[None]).astype(np.float32)
    picked = jnp.einsum("hdo,oqk->hdqk", rpb.astype(f32), jnp.asarray(onehot), precision=lax.Precision.HIGHEST)
    blocks = jnp.where(jnp.asarray(in_win), picked, NEG_INF)
    return jnp.concatenate([blocks[:, :-1], blocks[:, 1:]], axis=-1)


def _diff_position_features(pos, lane_in_half, first):
    f = lane_in_half - first
    lo = jnp.bitwise_and(pos, DIFF_POS_RADIX - 1)
    hi = pos - lo
    val = jnp.where(jnp.bitwise_and(f, 1) == 0, hi, lo)
    return jnp.where((f >= 0) & (f < 2 * DIFF_SLOPE_PARTS), val, 0).astype(f32)


def _diff_kernel(lam_ref, q_ref, qnext_ref, k_ref, v_ref, cvec_ref, nw_ref, o_ref,
                 kaug_ref, vaug_ref, qv_ref, s_even_ref, s_odd_ref, m_even_ref, m_odd_ref, *, S, out_scale):
    tq = DIFF_TQ
    nkt = S // tq
    qi = pl.program_id(2)
    lane = lax.broadcasted_iota(jnp.int32, (1, LANES), 1)
    lo = lane < HEAD_DIM
    lane_in_half = jnp.bitwise_and(lane, HEAD_DIM - 1)
    cq = cvec_ref[0:1, :]
    ck = cvec_ref[1:2, :]

    def key_rows(tile, r):
        t = jnp.bitwise_and(tile + r, nkt - 1)
        return pl.ds(pl.multiple_of(t * tq, tq), tq)

    def halves_max(x):
        return jnp.maximum(x[:, :LANES], x[:, LANES:])

    def pipeline(q_blk_ref, nxt_tile, nxt_s_ref, nxt_m_ref, cur_tile, cur_s_ref, cur_m_ref, out_ref):
        if q_blk_ref is not None:
            qs = (q_blk_ref[...].astype(f32) * (HEAD_DIM ** -0.5 * LOG2E)).astype(bf16)
            qpos = nxt_tile * tq + lax.broadcasted_iota(jnp.int32, (tq, 1), 0)
            fq = _diff_position_features(qpos, lane_in_half, 0) + cq
            zero = jnp.zeros_like(qs)
            for v, f in enumerate((fq.astype(bf16), zero, (-fq).astype(bf16))):
                qv_ref[v] = jnp.where(lo, qs, f)
                qv_ref[3 + v] = jnp.where(lo, f, qs)
            d = (lax.broadcasted_iota(jnp.int32, (tq, tq), 0)
                 - lax.broadcasted_iota(jnp.int32, (tq, tq), 1)).astype(f32)
            diag_bias = -cvec_ref[2:3, 0:1] * jnp.abs(d)

        if cur_s_ref is not None:
            m_cur = [jnp.max(cur_m_ref[c], axis=1, keepdims=True) for c in range(2)]
            acc = [None, None]
        m_acc = [None, None]
        for r in range(nkt):
            if cur_s_ref is not None:
                v_ones = vaug_ref[key_rows(cur_tile, r), :]
                for c in range(2):
                    e = jnp.exp2(cur_s_ref[c, r] - m_cur[c]).astype(bf16)
                    pv = _dot(e, v_ones)
                    acc[c] = pv if acc[c] is None else acc[c] + pv
            if q_blk_ref is None:
                continue
            rows = key_rows(nxt_tile, r)
            variant = 1 if r == 0 else jnp.where(nxt_tile + r >= nkt, 0, 2)
            for c in range(2):
                s = _nt_dot(qv_ref[3 * c + variant], kaug_ref[c, rows, :])
                if r == 0:
                    s = s + diag_bias
                nxt_s_ref[c, r] = s
                m_acc[c] = halves_max(s) if r == 0 else jnp.maximum(m_acc[c], halves_max(s))
        if q_blk_ref is not None:
            for c in range(2):
                nxt_m_ref[c] = m_acc[c]
        if cur_s_ref is None:
            return

        r1 = 1.0 / acc[0][:, DIFF_V_DIM:DIFF_V_DIM + 1]
        r2 = lam_ref[0] / acc[1][:, DIFF_V_DIM:DIFF_V_DIM + 1]
        out = acc[0][:, :DIFF_V_DIM] * r1 - acc[1][:, :DIFF_V_DIM] * r2
        y = out * lax.rsqrt(jnp.mean(out * out, axis=-1, keepdims=True) + NORM_EPS)
        out_ref[...] = (y * nw_ref[...] * out_scale).astype(out_ref.dtype)

    @pl.when(qi == 0)
    def _():
        kpos = lax.broadcasted_iota(jnp.int32, (S, 1), 0)
        fk = (_diff_position_features(kpos, lane_in_half, 2 * DIFF_SLOPE_PARTS) + ck).astype(bf16)
        k = k_ref[...]
        kaug_ref[0] = jnp.where(lo, k, fk)
        kaug_ref[1] = jnp.where(lo, fk, k)
        vaug_ref[:, 0:DIFF_V_DIM] = v_ref[...]
        vaug_ref[:, DIFF_V_DIM:] = jnp.broadcast_to(jnp.where(lane == 0, 1.0, 0.0).astype(bf16), (S, LANES))
        pipeline(q_ref.at[0:tq], 0, s_even_ref, m_even_ref, None, None, None, None)

    even = 2 * qi
    last = nkt // 2 - 1
    pipeline(q_ref.at[tq:2 * tq], even + 1, s_odd_ref, m_odd_ref, even, s_even_ref, m_even_ref, o_ref.at[0:tq])

    @pl.when(qi < last)
    def _():
        pipeline(qnext_ref, even + 2, s_even_ref, m_even_ref, even + 1, s_odd_ref, m_odd_ref,
                 o_ref.at[tq:2 * tq])

    @pl.when(qi == last)
    def _():
        pipeline(None, None, None, None, even + 1, s_odd_ref, m_odd_ref, o_ref.at[tq:2 * tq])


def diff_slope_vectors(slopes):
    c = slopes.astype(f32) * LOG2E
    parts = []
    rest = c
    for _ in range(DIFF_SLOPE_PARTS):
        p = rest.astype(bf16).astype(f32)
        parts.append(p)
        rest = rest - p
    parts = jnp.stack(parts, axis=1)
    lane_in_half = np.arange(LANES) % HEAD_DIM
    n = 2 * DIFF_SLOPE_PARTS
    q_sel = np.where((lane_in_half >= n) & (lane_in_half < 2 * n), (lane_in_half - n) // 2, -1)
    k_sel = np.where(lane_in_half < n, lane_in_half // 2, -1)
    zero = jnp.zeros((slopes.shape[0], 1), f32)
    padded = jnp.concatenate([parts, zero], axis=1)
    cq = padded[:, q_sel]
    ck = -padded[:, k_sel]
    return jnp.stack([cq, ck, jnp.broadcast_to(c[:, None], cq.shape)], axis=1)


def diff_attention(proj, cvec, lam, norm_w, *, B, S, lam_init):
    T = B * S
    tq = DIFF_TQ
    nq = S // tq
    assert S <= DIFF_POS_RADIX * DIFF_POS_RADIX and nq & (nq - 1) == 0 and nq % 2 == 0
    steps = nq // 2
    blk0 = DIFF_COL0 // LANES
    smem = pl.BlockSpec(memory_space=pltpu.SMEM)
    return pl.pallas_call(
        functools.partial(_diff_kernel, S=S, out_scale=1.0 - lam_init),
        grid=(B, DIFF_HEADS, steps),
        in_specs=[smem,
                  pl.BlockSpec((2 * tq, LANES), lambda b, h, i: (b * steps + i, blk0 + h)),
                  pl.BlockSpec((tq, LANES), lambda b, h, i: (b * nq + jnp.minimum(2 * i + 2, nq - 1), blk0 + h)),
                  pl.BlockSpec((S, LANES), lambda b, h, i: (b, blk0 + DIFF_HEADS + h)),
                  pl.BlockSpec((S, LANES), lambda b, h, i: (b, blk0 + 2 * DIFF_HEADS + h)),
                  pl.BlockSpec((None, 3, LANES), lambda b, h, i: (h, 0, 0)),
                  pl.BlockSpec((1, LANES), lambda b, h, i: (0, h))],
        out_specs=pl.BlockSpec((2 * tq, LANES), lambda b, h, i: (b * steps + i, h)),
        out_shape=jax.ShapeDtypeStruct((T, DIFF_W), bf16),
        scratch_shapes=[pltpu.VMEM((2, S, LANES), bf16),
                        pltpu.VMEM((S, 2 * LANES), bf16),
                        pltpu.VMEM((6, tq, LANES), bf16),
                        pltpu.VMEM((2, nq, tq, tq), f32),
                        pltpu.VMEM((2, nq, tq, tq), f32),
                        pltpu.VMEM((2, tq, LANES), f32),
                        pltpu.VMEM((2, tq, LANES), f32)],
        compiler_params=pltpu.CompilerParams(
            dimension_semantics=("parallel", "parallel", "arbitrary"), vmem_limit_bytes=VMEM_LIMIT),
        name="diff_attention",
    )(lam, proj, proj, proj, proj, cvec, norm_w.reshape(1, DIFF_W))


def _out_proj_kernel(x_ref, r_ref, n_ref, d_ref, w_ref, o_ref):
    acc = _dot(r_ref[...], w_ref[0:RET_W, :])
    acc += _dot(n_ref[...], w_ref[RET_W:RET_W + NA_W, :])
    acc += _dot(d_ref[...], w_ref[RET_W + NA_W:, :])
    o_ref[...] = x_ref[...] + acc


def out_proj(x, ret, na, diff, w_out, layer, *, tm=512):
    T, D = x.shape
    row = lambda i: (i, 0)
    return pl.pallas_call(
        _out_proj_kernel,
        grid=(T // tm,),
        in_specs=[pl.BlockSpec((tm, D), row),
                  pl.BlockSpec((tm, RET_W), row),
                  pl.BlockSpec((tm, NA_W), row),
                  pl.BlockSpec((tm, DIFF_W), row),
                  pl.BlockSpec((None, D, D), lambda i: (layer, 0, 0))],
        out_specs=pl.BlockSpec((tm, D), row),
        out_shape=jax.ShapeDtypeStruct((T, D), f32),
        compiler_params=pltpu.CompilerParams(
            dimension_semantics=("parallel",), vmem_limit_bytes=VMEM_LIMIT),
        name="out_proj",
    )(x, ret, na, diff, w_out)


def _rms(x):
    return x * lax.rsqrt(jnp.mean(x * x, axis=-1, keepdims=True) + NORM_EPS)


def _ffn_gate_kernel(x_ref, xp_ref, xn_ref, nw_ref, wa_ref, wg_ref, hw_ref, o_ref, h_ref, hh_ref,
                     *, S, tm):
    i = pl.program_id(0)

    @pl.when(pl.program_id(1) == 0)
    def _():
        nw = nw_ref[...]
        h_ref[...] = (_rms(x_ref[...]) * nw).astype(bf16)
        hh_ref[0:F32_SUBLANES, :] = (_rms(xp_ref[...]) * nw).astype(bf16)
        hh_ref[F32_SUBLANES:, :] = (_rms(xn_ref[...]) * nw).astype(bf16)

    hh = hh_ref[...]
    t0 = i * tm
    at_seq_start = (t0 % S) == 0
    at_seq_end = ((t0 + tm) % S) == 0
    pm = tm // FFN_ROW_PIECES
    edge = 2 * F32_SUBLANES
    span = edge + F32_SUBLANES
    row = lax.broadcasted_iota(jnp.int32, (pm, 1), 0)
    seam_row = lax.broadcasted_iota(jnp.int32, (span, 1), 0)

    def gate(g_up, g_mid, g_dn, a, cols):
        half = g_up * hw_ref[0:1, cols] + g_mid * hw_ref[1:2, cols] + g_dn * hw_ref[2:3, cols] + hw_ref[3:4, cols]
        u = half * (GELU_C1 + GELU_C3 * (half * half))
        ah = a * half
        return ah + ah * jnp.tanh(u)

    for c0 in range(0, o_ref.shape[1], MXU_WIDTH):
        cols = slice(c0, c0 + MXU_WIDTH)
        wg = wg_ref[:, cols]
        wa = wa_ref[:, cols]
        g_halo = _dot(hh, wg)
        above = jnp.where(at_seq_start, 0.0, g_halo[F32_SUBLANES - 1:F32_SUBLANES, :])
        below_tile = jnp.where(at_seq_end, 0.0, g_halo[F32_SUBLANES:F32_SUBLANES + 1, :])
        held = None
        for p in range(FFN_ROW_PIECES):
            r0 = p * pm
            g = _dot(h_ref[r0:r0 + pm, :], wg)
            a = _dot(h_ref[r0:r0 + pm, :], wa)
            if held is not None:
                g_seam, a_seam, seam_at = held
                dn = jnp.where(seam_row == span - 1, g[0:1, :], pltpu.roll(g_seam, span - 1, 0))
                seam = gate(pltpu.roll(g_seam, 1, 0), g_seam, dn, a_seam, cols)
                o_ref[seam_at - edge:seam_at, cols] = seam[span - edge:span].astype(o_ref.dtype)
            up = jnp.where(row == 0, above, pltpu.roll(g, 1, 0))
            dn = pltpu.roll(g, pm - 1, 0)
            if p == FFN_ROW_PIECES - 1:
                dn = jnp.where(row == pm - 1, below_tile, dn)
            o_ref[r0:r0 + pm, cols] = gate(up, g, dn, a, cols).astype(o_ref.dtype)
            above = g[pm - 1:pm, :]
            held = (g[pm - span:pm, :], a[pm - span:pm, :], r0 + pm)


def ffn_gate(x, nw, w_up, layer, conv_w, conv_b, *, S, tm=1024, tf=512):
    T, D = x.shape
    half_conv = 0.5 * jnp.concatenate([conv_w, conv_b.reshape(1, D_FF)], axis=0)
    nf = D_FF // tf
    rb = tm // F32_SUBLANES
    last_rb = T // F32_SUBLANES - 1
    assert S % tm == 0
    return pl.pallas_call(
        functools.partial(_ffn_gate_kernel, S=S, tm=tm),
        grid=(T // tm, nf),
        in_specs=[pl.BlockSpec((tm, D), lambda i, j: (i, 0)),
                  pl.BlockSpec((F32_SUBLANES, D), lambda i, j: (jnp.maximum(i * rb - 1, 0), 0)),
                  pl.BlockSpec((F32_SUBLANES, D), lambda i, j: (jnp.minimum((i + 1) * rb, last_rb), 0)),
                  pl.BlockSpec((1, D), lambda i, j: (0, 0)),
                  pl.BlockSpec((None, D, tf), lambda i, j: (layer, 0, j)),
                  pl.BlockSpec((None, D, tf), lambda i, j: (layer, 0, nf + j)),
                  pl.BlockSpec((4, tf), lambda i, j: (0, j))],
        out_specs=pl.BlockSpec((tm, tf), lambda i, j: (i, j)),
        out_shape=jax.ShapeDtypeStruct((T, D_FF), bf16),
        scratch_shapes=[pltpu.VMEM((tm, D), bf16), pltpu.VMEM((2 * F32_SUBLANES, D), bf16)],
        compiler_params=pltpu.CompilerParams(
            dimension_semantics=("parallel", "arbitrary"), vmem_limit_bytes=VMEM_LIMIT),
        name="ffn_gate",
    )(x, x, x, nw.reshape(1, D), w_up, w_up, half_conv)


def _ffn_down_kernel(x_ref, g_ref, wd_ref, fw_ref, o_ref, *, final_norm):
    k = pl.program_id(1)

    @pl.when(k == 0)
    def _():
        o_ref[...] = x_ref[...]

    o_ref[...] += _dot(g_ref[...], wd_ref[...])

    if final_norm:
        @pl.when(k == pl.num_programs(1) - 1)
        def _():
            x = o_ref[...]
            y = x * lax.rsqrt(jnp.mean(x * x, axis=-1, keepdims=True) + NORM_EPS)
            o_ref[...] = y * fw_ref[...]


def ffn_down(x, gated, w_down, layer, final_w, *, final_norm, tm=512, tk=2816):
    T, D = x.shape
    nk = D_FF // tk
    return pl.pallas_call(
        functools.partial(_ffn_down_kernel, final_norm=final_norm),
        grid=(T // tm, nk),
        in_specs=[pl.BlockSpec((tm, D), lambda i, k: (i, 0)),
                  pl.BlockSpec((tm, tk), lambda i, k: (i, k)),
                  pl.BlockSpec((None, tk, D), lambda i, k: (layer, k, 0)),
                  pl.BlockSpec((1, D), lambda i, k: (0, 0))],
        out_specs=pl.BlockSpec((tm, D), lambda i, k: (i, 0)),
        out_shape=jax.ShapeDtypeStruct((T, D), f32),
        compiler_params=pltpu.CompilerParams(
            dimension_semantics=("parallel", "arbitrary"), vmem_limit_bytes=VMEM_LIMIT),
        name="ffn_down",
    )(x, gated, w_down, final_w.reshape(1, D))


def _prepare_layer(l, norm1_w, ret_decay_fwd, ret_decay_bwd, ret_norm_w, na_rpb,
                   lq1, lk1, lq2, lk2, diff_norm_w, norm2_w, ffn_conv_w, ffn_conv_b):
    lam_init = 0.8 - 0.6 * math.exp(-0.3 * l)
    lam = (jnp.exp(jnp.sum(lq1[l].astype(f32) * lk1[l].astype(f32)))
           - jnp.exp(jnp.sum(lq2[l].astype(f32) * lk2[l].astype(f32))) + lam_init)
    return dict(
        norm1_w=norm1_w[l].astype(f32),
        lgf=jax.nn.log_sigmoid(ret_decay_fwd[l].astype(f32)),
        lgb=jax.nn.log_sigmoid(ret_decay_bwd[l].astype(f32)),
        ret_norm_w=ret_norm_w[l].astype(f32),
        na_bias=na_bias_table(na_rpb[l]),
        lam=lam.reshape(1), lam_init=lam_init, diff_norm_w=diff_norm_w[l].astype(f32),
        norm2_w=norm2_w[l].astype(f32), conv_w=ffn_conv_w[l].astype(f32), conv_b=ffn_conv_b[l].astype(f32))


def _prepare_weights(w_in, w_out, ffn_w_up, ffn_w_down):
    n0, d0 = 4 * RET_W, 4 * RET_W + 3 * NA_W
    w_in_perm = jnp.concatenate([w_in[:, :, n0:d0], w_in[:, :, d0:], w_in[:, :, :n0]], axis=2)
    return dict(w_in=w_in_perm.astype(bf16), w_out=w_out.astype(bf16),
                w_up=ffn_w_up.astype(bf16), w_down=ffn_w_down.astype(bf16))


def _trunk(x, layers, weights, final_norm_w, slope_vecs):
    B, S, D = x.shape
    x = x.reshape(B * S, D)
    for l, p in enumerate(layers):
        proj = norm_matmul(x, p["norm1_w"], weights["w_in"], l)
        ret = retention(proj, p["lgf"], p["lgb"], p["ret_norm_w"], B=B, S=S)
        na = neighbourhood_attention(proj, p["na_bias"], B=B, S=S)
        diff = diff_attention(proj, slope_vecs, p["lam"], p["diff_norm_w"], B=B, S=S, lam_init=p["lam_init"])
        x = out_proj(x, ret, na, diff, weights["w_out"], l)
        gated = ffn_gate(x, p["norm2_w"], weights["w_up"], l, p["conv_w"], p["conv_b"], S=S)
        x = ffn_down(x, gated, weights["w_down"], l, final_norm_w.astype(f32),
                     final_norm=(l == len(layers) - 1))
    return x.reshape(B, S, D)


def kernel(x_prompt, x_sample, norm1_w, w_in, ret_decay_fwd, ret_decay_bwd, ret_norm_w, na_rpb, diff_lambda_q1, diff_lambda_k1, diff_lambda_q2, diff_lambda_k2, diff_norm_w, w_out, norm2_w, ffn_w_up, ffn_conv_w, ffn_conv_b, ffn_w_down, final_norm_w):
    layers = [_prepare_layer(l, norm1_w, ret_decay_fwd, ret_decay_bwd, ret_norm_w, na_rpb,
                             diff_lambda_q1, diff_lambda_k1, diff_lambda_q2, diff_lambda_k2,
                             diff_norm_w, norm2_w, ffn_conv_w, ffn_conv_b)
              for l in range(w_in.shape[0])]
    weights = _prepare_weights(w_in, w_out, ffn_w_up, ffn_w_down)
    slopes = 2.0 ** (-8.0 * (jnp.arange(DIFF_HEADS, dtype=f32) + 1.0) / DIFF_HEADS)
    slope_vecs = diff_slope_vectors(slopes)
    y_prompt = _trunk(x_prompt, layers, weights, final_norm_w, slope_vecs)
    y_sample = _trunk(x_sample, layers, weights, final_norm_w, slope_vecs)
    return (y_prompt, y_sample)
```

```python
import functools
import math

import jax
import jax.numpy as jnp
import numpy as np
from jax import lax
from jax.experimental import pallas as pl
from jax.experimental.pallas import tpu as pltpu

D_MODEL = 2048
DEPTH = 2
HEAD_DIM = 64
RET_HEADS = 8
RET_W = RET_HEADS * HEAD_DIM
NA_HEADS = 12
NA_W = NA_HEADS * HEAD_DIM
GRID_W = 64
NA_KH = 8
NA_KW = 16
DIFF_HEADS = 6
DIFF_V_DIM = 2 * HEAD_DIM
DIFF_W = DIFF_HEADS * DIFF_V_DIM
D_IN = 4 * RET_W + 3 * NA_W + 3 * DIFF_W
D_FF = 5632
NORM_EPS = 1e-6
NEG_INF = -1e30

LANES = 128
F32_SUBLANES = 8
MXU_WIDTH = 256
RET_CHUNK = 256
RET_UNROLL = 8
FFN_ROW_PIECES = 2
NA_ROWS_PER_STEP = 4
DIFF_TQ = 256
DIFF_POS_RADIX = 64
DIFF_SLOPE_PARTS = 3
LOG2E = 1.4426950408889634
GELU_C1 = 2.0 * math.sqrt(2.0 / math.pi)
GELU_C3 = 8.0 * math.sqrt(2.0 / math.pi) * 0.044715
VMEM_LIMIT = 56 * 1024 * 1024

NA_COL0 = 0
DIFF_COL0 = 3 * NA_W
RET_COL0 = 3 * NA_W + 3 * DIFF_W

f32 = jnp.float32
bf16 = jnp.bfloat16


def _nt_dot(a, b):
    return lax.dot_general(a, b, (((1,), (1,)), ((), ())), preferred_element_type=f32)


def _tn_dot(a, b):
    return lax.dot_general(a, b, (((0,), (0,)), ((), ())), preferred_element_type=f32)


def _dot(a, b):
    return jnp.dot(a, b, preferred_element_type=f32)


def _norm_matmul_kernel(x_ref, nw_ref, w_ref, o_ref, h_ref):
    @pl.when(pl.program_id(1) == 0)
    def _():
        x = x_ref[...]
        y = x * lax.rsqrt(jnp.mean(x * x, axis=-1, keepdims=True) + NORM_EPS)
        h_ref[...] = (y * nw_ref[...]).astype(bf16)

    o_ref[...] = _dot(h_ref[...], w_ref[...]).astype(o_ref.dtype)


def norm_matmul(x, nw, w, layer, *, tm=512, tn=3328):
    T, D = x.shape
    N = w.shape[2]
    return pl.pallas_call(
        _norm_matmul_kernel,
        grid=(T // tm, N // tn),
        in_specs=[
            pl.BlockSpec((tm, D), lambda i, j: (i, 0)),
            pl.BlockSpec((1, D), lambda i, j: (0, 0)),
            pl.BlockSpec((None, D, tn), lambda i, j: (layer, 0, j)),
        ],
        out_specs=pl.BlockSpec((tm, tn), lambda i, j: (i, j)),
        out_shape=jax.ShapeDtypeStruct((T, N), bf16),
        scratch_shapes=[pltpu.VMEM((tm, D), bf16)],
        compiler_params=pltpu.CompilerParams(
            dimension_semantics=("parallel", "arbitrary"), vmem_limit_bytes=VMEM_LIMIT),
        name="norm_matmul",
    )(x, nw.reshape(1, D), w)


def _retention_kernel(lgf_ref, lgb_ref, q_ref, k_ref, v_ref, g_ref, nw_ref, o_ref, acc_ref, *, S):
    C = RET_CHUNK
    n = S // C
    pair = pl.program_id(1)
    scale = HEAD_DIM ** -0.5

    lane = lax.broadcasted_iota(jnp.int32, (1, LANES), 1)
    lo = lane < HEAD_DIM
    lgf = jnp.where(lo, lgf_ref[2 * pair], lgf_ref[2 * pair + 1])
    lgb = jnp.where(lo, lgb_ref[2 * pair], lgb_ref[2 * pair + 1])

    pos = lax.broadcasted_iota(jnp.int32, (C, 1), 0).astype(f32)
    qdec_f = jnp.exp(lgf * (pos + 1.0))
    kdec_f = jnp.exp(lgf * (C - 1.0 - pos)) * scale
    cdec_f = jnp.exp(lgf * float(C))
    qdec_b = jnp.exp(lgb * (C - pos))
    kdec_b = jnp.exp(lgb * pos) * scale
    cdec_b = jnp.exp(lgb * float(C))

    ri = lax.broadcasted_iota(jnp.int32, (C, C), 0)
    ci = lax.broadcasted_iota(jnp.int32, (C, C), 1)
    delta = (ri - ci).astype(f32)

    def decay_mat(h):
        fwd = jnp.exp(lgf_ref[2 * pair + h] * jnp.maximum(delta, 0.0))
        bwd = jnp.exp(lgb_ref[2 * pair + h] * jnp.maximum(-delta, 0.0))
        return jnp.where(delta >= 0, fwd, bwd) * scale

    dmat0 = decay_mat(0)
    dmat1 = decay_mat(1)

    r2 = lax.broadcasted_iota(jnp.int32, (LANES, LANES), 0) // HEAD_DIM
    c2 = lax.broadcasted_iota(jnp.int32, (LANES, LANES), 1) // HEAD_DIM
    same_head = r2 == c2
    avg = jnp.where(same_head, 1.0 / HEAD_DIM, 0.0).astype(bf16)

    def chunk(ref, c):
        return ref[pl.ds(pl.multiple_of(c * C, C), C), :]

    def fwd_body(c, state):
        qc, kc, vc = chunk(q_ref, c), chunk(k_ref, c), chunk(v_ref, c)
        zero = jnp.zeros_like(qc)
        in0 = _nt_dot(jnp.where(lo, qc, zero), kc)
        in1 = _nt_dot(jnp.where(lo, zero, qc), kc)
        w = jnp.concatenate([(in0 * dmat0).astype(bf16), (in1 * dmat1).astype(bf16)], axis=1)
        vv = jnp.concatenate([jnp.where(lo, vc, zero), jnp.where(lo, zero, vc)], axis=0)
        intra = _dot(w, vv)
        qd = (qc.astype(f32) * qdec_f).astype(bf16)
        cross = _dot(qd, state.astype(bf16))
        acc_ref[pl.ds(pl.multiple_of(c * C, C), C), :] = intra + cross
        kd = (kc.astype(f32) * kdec_f).astype(bf16)
        upd = _tn_dot(kd, vc)
        return state * cdec_f + jnp.where(same_head, upd, 0.0)

    lax.fori_loop(0, n, fwd_body, jnp.zeros((LANES, LANES), f32), unroll=RET_UNROLL)

    def lane_mean(y):
        hi = y.astype(bf16)
        lo_part = (y - hi.astype(f32)).astype(bf16)
        return _dot(hi, avg) + _dot(lo_part, avg)

    def bwd_body(t, state):
        c = n - 1 - t
        qc, kc, vc = chunk(q_ref, c), chunk(k_ref, c), chunk(v_ref, c)
        qd = (qc.astype(f32) * qdec_b).astype(bf16)
        y = chunk(acc_ref, c) + _dot(qd, state.astype(bf16))
        y = y - lane_mean(y)
        y = y * lax.rsqrt(lane_mean(y * y) + NORM_EPS)
        gc = chunk(g_ref, c).astype(f32)
        out = jax.nn.silu(gc) * (y * nw_ref[...])
        o_ref[pl.ds(pl.multiple_of(c * C, C), C), :] = out.astype(o_ref.dtype)
        kd = (kc.astype(f32) * kdec_b).astype(bf16)
        upd = _tn_dot(kd, vc)
        return state * cdec_b + jnp.where(same_head, upd, 0.0)

    lax.fori_loop(0, n, bwd_body, jnp.zeros((LANES, LANES), f32), unroll=RET_UNROLL)


def retention(proj, lgf, lgb, norm_w, *, B, S):
    T = B * S
    blk0 = RET_COL0 // LANES
    npair = RET_HEADS // 2

    def col(seg):
        return lambda b, p: (b, blk0 + seg * npair + p)

    smem = pl.BlockSpec(memory_space=pltpu.SMEM)
    return pl.pallas_call(
        functools.partial(_retention_kernel, S=S),
        grid=(B, npair),
        in_specs=[smem, smem,
                  pl.BlockSpec((S, LANES), col(0)),
                  pl.BlockSpec((S, LANES), col(1)),
                  pl.BlockSpec((S, LANES), col(2)),
                  pl.BlockSpec((S, LANES), col(3)),
                  pl.BlockSpec((1, LANES), lambda b, p: (0, p))],
        out_specs=pl.BlockSpec((S, LANES), lambda b, p: (b, p)),
        out_shape=jax.ShapeDtypeStruct((T, RET_W), bf16),
        scratch_shapes=[pltpu.VMEM((S, LANES), f32)],
        compiler_params=pltpu.CompilerParams(
            dimension_semantics=("parallel", "parallel"), vmem_limit_bytes=VMEM_LIMIT),
        name="retention",
    )(lgf, lgb, proj, proj, proj, proj, norm_w.reshape(1, RET_W))


def _na_window_start(r, rows):
    return jnp.clip(r - NA_KH // 2, 0, rows - NA_KH)


def _na_kernel(q_ref, k_ref, v_ref, bias_ref, o_ref, *, rows):
    g = pl.program_id(1)
    nwin = NA_KH * GRID_W
    lane = lax.broadcasted_iota(jnp.int32, (1, LANES), 1)
    lo = lane < HEAD_DIM
    scale = HEAD_DIM ** -0.5
    npair = NA_HEADS // 2
    for j in range(NA_ROWS_PER_STEP):
        r = g * NA_ROWS_PER_STEP + j
        start = pl.multiple_of(_na_window_start(r, rows) * GRID_W, GRID_W)
        tok = slice(j * GRID_W, (j + 1) * GRID_W)
        parts = []
        for p in range(npair):
            cols = slice(p * LANES, (p + 1) * LANES)
            qp = (q_ref[tok, cols].astype(f32) * scale).astype(bf16)
            kp = k_ref[pl.ds(start, nwin), cols]
            zero = jnp.zeros_like(qp)
            parts.append(_nt_dot(jnp.where(lo, qp, zero), kp))
            parts.append(_nt_dot(jnp.where(lo, zero, qp), kp))
        d0 = NA_KH - 1 - (r - _na_window_start(r, rows))
        bias = jnp.concatenate(
            [jnp.concatenate([bias_ref[h, d0 + w] for w in range(0, NA_KH, 2)], axis=1)
             for h in range(NA_HEADS)], axis=0)
        s = jnp.concatenate(parts, axis=0) + bias
        e = jnp.exp(s - jnp.max(s, axis=-1, keepdims=True))
        inv = 1.0 / jnp.sum(e, axis=-1, keepdims=True)
        eb = e.astype(bf16)
        for p in range(npair):
            cols = slice(p * LANES, (p + 1) * LANES)
            vp = v_ref[pl.ds(start, nwin), cols]
            r0, r1, r2 = 2 * p * GRID_W, (2 * p + 1) * GRID_W, (2 * p + 2) * GRID_W
            even = _dot(eb[r0:r1], vp) * inv[r0:r1]
            odd = _dot(eb[r1:r2], vp) * inv[r1:r2]
            o_ref[tok, cols] = jnp.where(lo, even, odd).astype(o_ref.dtype)


def neighbourhood_attention(proj, bias_table, *, B, S):
    T = B * S
    rows = S // GRID_W
    nwin = NA_KH * GRID_W
    R = NA_ROWS_PER_STEP
    steps = rows // R
    assert rows % R == 0 and rows >= NA_KH

    return pl.pallas_call(
        functools.partial(_na_kernel, rows=rows),
        grid=(B, steps),
        in_specs=[pl.BlockSpec((R * GRID_W, NA_W), lambda b, g: (b * steps + g, 0)),
                  pl.BlockSpec((S, NA_W), lambda b, g: (b, 1)),
                  pl.BlockSpec((S, NA_W), lambda b, g: (b, 2)),
                  pl.BlockSpec(bias_table.shape, lambda b, g: (0, 0, 0, 0))],
        out_specs=pl.BlockSpec((R * GRID_W, NA_W), lambda b, g: (b * steps + g, 0)),
        out_shape=jax.ShapeDtypeStruct((T, NA_W), bf16),
        compiler_params=pltpu.CompilerParams(
            dimension_semantics=("parallel", "arbitrary"), vmem_limit_bytes=VMEM_LIMIT),
        name="neighbourhood_attention",
    )(proj, proj, proj, bias_table)


def na_bias_table(rpb):
    qcol = np.arange(GRID_W)[:, None]
    kcol = np.arange(GRID_W)[None, :]
    cs = np.clip(qcol - NA_KW // 2, 0, GRID_W - NA_KW)
    in_win = (kcol >= cs) & (kcol < cs + NA_KW)
    dc = np.clip(kcol - qcol + NA_KW - 1, 0, 2 * NA_KW - 2)
    blocks = jnp.where(jnp.asarray(in_win), rpb.astype(f32)[:, :, dc], NEG_INF)
    return jnp.concatenate([blocks[:, :-1], blocks[:, 1:]], axis=-1)


def _diff_position_features(pos, lane_in_half, first):
    f = lane_in_half - first
    lo = jnp.bitwise_and(pos, DIFF_POS_RADIX - 1)
    hi = pos - lo
    val = jnp.where(jnp.bitwise_and(f, 1) == 0, hi, lo)
    return jnp.where((f >= 0) & (f < 2 * DIFF_SLOPE_PARTS), val, 0).astype(f32)


def _diff_kernel(lam_ref, q_ref, qnext_ref, k_ref, v_ref, cvec_ref, nw_ref, o_ref,
                 kaug_ref, vaug_ref, qv_ref, s_even_ref, s_odd_ref, m_even_ref, m_odd_ref, *, S, out_scale):
    tq = DIFF_TQ
    nkt = S // tq
    qi = pl.program_id(2)
    lane = lax.broadcasted_iota(jnp.int32, (1, LANES), 1)
    lo = lane < HEAD_DIM
    lane_in_half = jnp.bitwise_and(lane, HEAD_DIM - 1)
    cq = cvec_ref[0:1, :]
    ck = cvec_ref[1:2, :]

    def key_rows(tile, r):
        t = jnp.bitwise_and(tile + r, nkt - 1)
        return pl.ds(pl.multiple_of(t * tq, tq), tq)

    def halves_max(x):
        return jnp.maximum(x[:, :LANES], x[:, LANES:])

    def pipeline(q_blk_ref, nxt_tile, nxt_s_ref, nxt_m_ref, cur_tile, cur_s_ref, cur_m_ref, out_ref):
        if q_blk_ref is not None:
            qs = (q_blk_ref[...].astype(f32) * (HEAD_DIM ** -0.5 * LOG2E)).astype(bf16)
            qpos = nxt_tile * tq + lax.broadcasted_iota(jnp.int32, (tq, 1), 0)
            fq = _diff_position_features(qpos, lane_in_half, 0) + cq
            zero = jnp.zeros_like(qs)
            for v, f in enumerate((fq.astype(bf16), zero, (-fq).astype(bf16))):
                qv_ref[v] = jnp.where(lo, qs, f)
                qv_ref[3 + v] = jnp.where(lo, f, qs)
            d = (lax.broadcasted_iota(jnp.int32, (tq, tq), 0)
                 - lax.broadcasted_iota(jnp.int32, (tq, tq), 1)).astype(f32)
            diag_bias = -cvec_ref[2:3, 0:1] * jnp.abs(d)

        if cur_s_ref is not None:
            m_cur = [jnp.max(cur_m_ref[c], axis=1, keepdims=True) for c in range(2)]
            acc = [None, None]
        m_acc = [None, None]
        for r in range(nkt):
            if cur_s_ref is not None:
                v_ones = vaug_ref[key_rows(cur_tile, r), :]
                for c in range(2):
                    e = jnp.exp2(cur_s_ref[c, r] - m_cur[c]).astype(bf16)
                    pv = _dot(e, v_ones)
                    acc[c] = pv if acc[c] is None else acc[c] + pv
            if q_blk_ref is None:
                continue
            rows = key_rows(nxt_tile, r)
            variant = 1 if r == 0 else jnp.where(nxt_tile + r >= nkt, 0, 2)
            for c in range(2):
                s = _nt_dot(qv_ref[3 * c + variant], kaug_ref[c, rows, :])
                if r == 0:
                    s = s + diag_bias
                nxt_s_ref[c, r] = s
                m_acc[c] = halves_max(s) if r == 0 else jnp.maximum(m_acc[c], halves_max(s))
        if q_blk_ref is not None:
            for c in range(2):
                nxt_m_ref[c] = m_acc[c]
        if cur_s_ref is None:
            return

        r1 = 1.0 / acc[0][:, DIFF_V_DIM:DIFF_V_DIM + 1]
        r2 = lam_ref[0] / acc[1][:, DIFF_V_DIM:DIFF_V_DIM + 1]
        out = acc[0][:, :DIFF_V_DIM] * r1 - acc[1][:, :DIFF_V_DIM] * r2
        y = out * lax.rsqrt(jnp.mean(out * out, axis=-1, keepdims=True) + NORM_EPS)
        out_ref[...] = (y * nw_ref[...] * out_scale).astype(out_ref.dtype)

    @pl.when(qi == 0)
    def _():
        kpos = lax.broadcasted_iota(jnp.int32, (S, 1), 0)
        fk = (_diff_position_features(kpos, lane_in_half, 2 * DIFF_SLOPE_PARTS) + ck).astype(bf16)
        k = k_ref[...]
        kaug_ref[0] = jnp.where(lo, k, fk)
        kaug_ref[1] = jnp.where(lo, fk, k)
        vaug_ref[:, 0:DIFF_V_DIM] = v_ref[...]
        vaug_ref[:, DIFF_V_DIM:] = jnp.broadcast_to(jnp.where(lane == 0, 1.0, 0.0).astype(bf16), (S, LANES))
        pipeline(q_ref.at[0:tq], 0, s_even_ref, m_even_ref, None, None, None, None)

    even = 2 * qi
    last = nkt // 2 - 1
    pipeline(q_ref.at[tq:2 * tq], even + 1, s_odd_ref, m_odd_ref, even, s_even_ref, m_even_ref, o_ref.at[0:tq])

    @pl.when(qi < last)
    def _():
        pipeline(qnext_ref, even + 2, s_even_ref, m_even_ref, even + 1, s_odd_ref, m_odd_ref,
                 o_ref.at[tq:2 * tq])

    @pl.when(qi == last)
    def _():
        pipeline(None, None, None, None, even + 1, s_odd_ref, m_odd_ref, o_ref.at[tq:2 * tq])


def diff_slope_vectors(slopes):
    c = slopes.astype(f32) * LOG2E
    parts = []
    rest = c
    for _ in range(DIFF_SLOPE_PARTS):
        p = rest.astype(bf16).astype(f32)
        parts.append(p)
        rest = rest - p
    parts = jnp.stack(parts, axis=1)
    lane_in_half = np.arange(LANES) % HEAD_DIM
    n = 2 * DIFF_SLOPE_PARTS
    q_sel = np.where((lane_in_half >= n) & (lane_in_half < 2 * n), (lane_in_half - n) // 2, -1)
    k_sel = np.where(lane_in_half < n, lane_in_half // 2, -1)
    zero = jnp.zeros((slopes.shape[0], 1), f32)
    padded = jnp.concatenate([parts, zero], axis=1)
    cq = padded[:, q_sel]
    ck = -padded[:, k_sel]
    return jnp.stack([cq, ck, jnp.broadcast_to(c[:, None], cq.shape)], axis=1)


def diff_attention(proj, cvec, lam, norm_w, *, B, S, lam_init):
    T = B * S
    tq = DIFF_TQ
    nq = S // tq
    assert S <= DIFF_POS_RADIX * DIFF_POS_RADIX and nq & (nq - 1) == 0 and nq % 2 == 0
    steps = nq // 2
    blk0 = DIFF_COL0 // LANES
    smem = pl.BlockSpec(memory_space=pltpu.SMEM)
    return pl.pallas_call(
        functools.partial(_diff_kernel, S=S, out_scale=1.0 - lam_init),
        grid=(B, DIFF_HEADS, steps),
        in_specs=[smem,
                  pl.BlockSpec((2 * tq, LANES), lambda b, h, i: (b * steps + i, blk0 + h)),
                  pl.BlockSpec((tq, LANES), lambda b, h, i: (b * nq + jnp.minimum(2 * i + 2, nq - 1), blk0 + h)),
                  pl.BlockSpec((S, LANES), lambda b, h, i: (b, blk0 + DIFF_HEADS + h)),
                  pl.BlockSpec((S, LANES), lambda b, h, i: (b, blk0 + 2 * DIFF_HEADS + h)),
                  pl.BlockSpec((None, 3, LANES), lambda b, h, i: (h, 0, 0)),
                  pl.BlockSpec((1, LANES), lambda b, h, i: (0, h))],
        out_specs=pl.BlockSpec((2 * tq, LANES), lambda b, h, i: (b * steps + i, h)),
        out_shape=jax.ShapeDtypeStruct((T, DIFF_W), bf16),
        scratch_shapes=[pltpu.VMEM((2, S, LANES), bf16),
                        pltpu.VMEM((S, 2 * LANES), bf16),
                        pltpu.VMEM((6, tq, LANES), bf16),
                        pltpu.VMEM((2, nq, tq, tq), f32),
                        pltpu.VMEM((2, nq, tq, tq), f32),
                        pltpu.VMEM((2, tq, LANES), f32),
                        pltpu.VMEM((2, tq, LANES), f32)],
        compiler_params=pltpu.CompilerParams(
            dimension_semantics=("parallel", "parallel", "arbitrary"), vmem_limit_bytes=VMEM_LIMIT),
        name="diff_attention",
    )(lam, proj, proj, proj, proj, cvec, norm_w.reshape(1, DIFF_W))


def _out_proj_kernel(x_ref, r_ref, n_ref, d_ref, w_ref, o_ref):
    acc = _dot(r_ref[...], w_ref[0:RET_W, :])
    acc += _dot(n_ref[...], w_ref[RET_W:RET_W + NA_W, :])
    acc += _dot(d_ref[...], w_ref[RET_W + NA_W:, :])
    o_ref[...] = x_ref[...] + acc


def out_proj(x, ret, na, diff, w_out, layer, *, tm=512):
    T, D = x.shape
    row = lambda i: (i, 0)
    return pl.pallas_call(
        _out_proj_kernel,
        grid=(T // tm,),
        in_specs=[pl.BlockSpec((tm, D), row),
                  pl.BlockSpec((tm, RET_W), row),
                  pl.BlockSpec((tm, NA_W), row),
                  pl.BlockSpec((tm, DIFF_W), row),
                  pl.BlockSpec((None, D, D), lambda i: (layer, 0, 0))],
        out_specs=pl.BlockSpec((tm, D), row),
        out_shape=jax.ShapeDtypeStruct((T, D), f32),
        compiler_params=pltpu.CompilerParams(
            dimension_semantics=("parallel",), vmem_limit_bytes=VMEM_LIMIT),
        name="out_proj",
    )(x, ret, na, diff, w_out)


def _rms(x):
    return x * lax.rsqrt(jnp.mean(x * x, axis=-1, keepdims=True) + NORM_EPS)


def _ffn_gate_kernel(x_ref, xp_ref, xn_ref, nw_ref, wa_ref, wg_ref, hw_ref, o_ref, h_ref, hh_ref,
                     *, S, tm):
    i = pl.program_id(0)

    @pl.when(pl.program_id(1) == 0)
    def _():
        nw = nw_ref[...]
        h_ref[...] = (_rms(x_ref[...]) * nw).astype(bf16)
        hh_ref[0:F32_SUBLANES, :] = (_rms(xp_ref[...]) * nw).astype(bf16)
        hh_ref[F32_SUBLANES:, :] = (_rms(xn_ref[...]) * nw).astype(bf16)

    hh = hh_ref[...]
    t0 = i * tm
    at_seq_start = (t0 % S) == 0
    at_seq_end = ((t0 + tm) % S) == 0
    pm = tm // FFN_ROW_PIECES
    edge = 2 * F32_SUBLANES
    span = edge + F32_SUBLANES
    row = lax.broadcasted_iota(jnp.int32, (pm, 1), 0)
    seam_row = lax.broadcasted_iota(jnp.int32, (span, 1), 0)

    def gate(g_up, g_mid, g_dn, a, cols):
        half = g_up * hw_ref[0:1, cols] + g_mid * hw_ref[1:2, cols] + g_dn * hw_ref[2:3, cols] + hw_ref[3:4, cols]
        u = half * (GELU_C1 + GELU_C3 * (half * half))
        ah = a * half
        return ah + ah * jnp.tanh(u)

    for c0 in range(0, o_ref.shape[1], MXU_WIDTH):
        cols = slice(c0, c0 + MXU_WIDTH)
        wg = wg_ref[:, cols]
        wa = wa_ref[:, cols]
        held = None
        for p in range(FFN_ROW_PIECES):
            r0 = p * pm
            if p == 0:
                g_ext = _dot(jnp.concatenate([hh, h_ref[0:pm, :]], axis=0), wg)
                g_halo, g = g_ext[0:edge, :], g_ext[edge:, :]
                above = jnp.where(at_seq_start, 0.0, g_halo[F32_SUBLANES - 1:F32_SUBLANES, :])
                below_tile = jnp.where(at_seq_end, 0.0, g_halo[F32_SUBLANES:F32_SUBLANES + 1, :])
            else:
                g = _dot(h_ref[r0:r0 + pm, :], wg)
            a = _dot(h_ref[r0:r0 + pm, :], wa)
            if held is not None:
                g_seam, a_seam, seam_at = held
                dn = jnp.where(seam_row == span - 1, g[0:1, :], pltpu.roll(g_seam, span - 1, 0))
                seam = gate(pltpu.roll(g_seam, 1, 0), g_seam, dn, a_seam, cols)
                o_ref[seam_at - edge:seam_at, cols] = seam[span - edge:span].astype(o_ref.dtype)
            up = jnp.where(row == 0, above, pltpu.roll(g, 1, 0))
            dn = pltpu.roll(g, pm - 1, 0)
            if p == FFN_ROW_PIECES - 1:
                dn = jnp.where(row == pm - 1, below_tile, dn)
            o_ref[r0:r0 + pm, cols] = gate(up, g, dn, a, cols).astype(o_ref.dtype)
            above = g[pm - 1:pm, :]
            held = (g[pm - span:pm, :], a[pm - span:pm, :], r0 + pm)


def ffn_gate(x, nw, w_up, layer, conv_w, conv_b, *, S, tm=1024, tf=512):
    T, D = x.shape
    half_conv = 0.5 * jnp.concatenate([conv_w, conv_b.reshape(1, D_FF)], axis=0)
    nf = D_FF // tf
    rb = tm // F32_SUBLANES
    last_rb = T // F32_SUBLANES - 1
    assert S % tm == 0
    return pl.pallas_call(
        functools.partial(_ffn_gate_kernel, S=S, tm=tm),
        grid=(T // tm, nf),
        in_specs=[pl.BlockSpec((tm, D), lambda i, j: (i, 0)),
                  pl.BlockSpec((F32_SUBLANES, D), lambda i, j: (jnp.maximum(i * rb - 1, 0), 0)),
                  pl.BlockSpec((F32_SUBLANES, D), lambda i, j: (jnp.minimum((i + 1) * rb, last_rb), 0)),
                  pl.BlockSpec((1, D), lambda i, j: (0, 0)),
                  pl.BlockSpec((None, D, tf), lambda i, j: (layer, 0, j)),
                  pl.BlockSpec((None, D, tf), lambda i, j: (layer, 0, nf + j)),
                  pl.BlockSpec((4, tf), lambda i, j: (0, j))],
        out_specs=pl.BlockSpec((tm, tf), lambda i, j: (i, j)),
        out_shape=jax.ShapeDtypeStruct((T, D_FF), bf16),
        scratch_shapes=[pltpu.VMEM((tm, D), bf16), pltpu.VMEM((2 * F32_SUBLANES, D), bf16)],
        compiler_params=pltpu.CompilerParams(
            dimension_semantics=("parallel", "arbitrary"), vmem_limit_bytes=VMEM_LIMIT),
        name="ffn_gate",
    )(x, x, x, nw.reshape(1, D), w_up, w_up, half_conv)


def _ffn_down_kernel(x_ref, g_ref, wd_ref, fw_ref, o_ref, *, final_norm):
    k = pl.program_id(1)

    @pl.when(k == 0)
    def _():
        o_ref[...] = x_ref[...]

    o_ref[...] += _dot(g_ref[...], wd_ref[...])

    if final_norm:
        @pl.when(k == pl.num_programs(1) - 1)
        def _():
            x = o_ref[...]
            y = x * lax.rsqrt(jnp.mean(x * x, axis=-1, keepdims=True) + NORM_EPS)
            o_ref[...] = y * fw_ref[...]


def ffn_down(x, gated, w_down, layer, final_w, *, final_norm, tm=512, tk=2816):
    T, D = x.shape
    nk = D_FF // tk
    return pl.pallas_call(
        functools.partial(_ffn_down_kernel, final_norm=final_norm),
        grid=(T // tm, nk),
        in_specs=[pl.BlockSpec((tm, D), lambda i, k: (i, 0)),
                  pl.BlockSpec((tm, tk), lambda i, k: (i, k)),
                  pl.BlockSpec((None, tk, D), lambda i, k: (layer, k, 0)),
                  pl.BlockSpec((1, D), lambda i, k: (0, 0))],
        out_specs=pl.BlockSpec((tm, D), lambda i, k: (i, 0)),
        out_shape=jax.ShapeDtypeStruct((T, D), f32),
        compiler_params=pltpu.CompilerParams(
            dimension_semantics=("parallel", "arbitrary"), vmem_limit_bytes=VMEM_LIMIT),
        name="ffn_down",
    )(x, gated, w_down, final_w.reshape(1, D))


def _prepare_layer(l, norm1_w, ret_decay_fwd, ret_decay_bwd, ret_norm_w, na_rpb,
                   lq1, lk1, lq2, lk2, diff_norm_w, norm2_w, ffn_conv_w, ffn_conv_b):
    lam_init = 0.8 - 0.6 * math.exp(-0.3 * l)
    lam = (jnp.exp(jnp.sum(lq1[l].astype(f32) * lk1[l].astype(f32)))
           - jnp.exp(jnp.sum(lq2[l].astype(f32) * lk2[l].astype(f32))) + lam_init)
    return dict(
        norm1_w=norm1_w[l].astype(f32),
        lgf=jax.nn.log_sigmoid(ret_decay_fwd[l].astype(f32)),
        lgb=jax.nn.log_sigmoid(ret_decay_bwd[l].astype(f32)),
        ret_norm_w=ret_norm_w[l].astype(f32),
        na_bias=na_bias_table(na_rpb[l]),
        lam=lam.reshape(1), lam_init=lam_init, diff_norm_w=diff_norm_w[l].astype(f32),
        norm2_w=norm2_w[l].astype(f32), conv_w=ffn_conv_w[l].astype(f32), conv_b=ffn_conv_b[l].astype(f32))


def _prepare_weights(w_in, w_out, ffn_w_up, ffn_w_down):
    n0, d0 = 4 * RET_W, 4 * RET_W + 3 * NA_W
    w_in_perm = jnp.concatenate([w_in[:, :, n0:d0], w_in[:, :, d0:], w_in[:, :, :n0]], axis=2)
    return dict(w_in=w_in_perm.astype(bf16), w_out=w_out.astype(bf16),
                w_up=ffn_w_up.astype(bf16), w_down=ffn_w_down.astype(bf16))


def _trunk(x, layers, weights, final_norm_w, slope_vecs):
    B, S, D = x.shape
    x = x.reshape(B * S, D)
    for l, p in enumerate(layers):
        proj = norm_matmul(x, p["norm1_w"], weights["w_in"], l)
        ret = retention(proj, p["lgf"], p["lgb"], p["ret_norm_w"], B=B, S=S)
        na = neighbourhood_attention(proj, p["na_bias"], B=B, S=S)
        diff = diff_attention(proj, slope_vecs, p["lam"], p["diff_norm_w"], B=B, S=S, lam_init=p["lam_init"])
        x = out_proj(x, ret, na, diff, weights["w_out"], l)
        gated = ffn_gate(x, p["norm2_w"], weights["w_up"], l, p["conv_w"], p["conv_b"], S=S)
        x = ffn_down(x, gated, weights["w_down"], l, final_norm_w.astype(f32),
                     final_norm=(l == len(layers) - 1))
    return x.reshape(B, S, D)


def kernel(x_prompt, x_sample, norm1_w, w_in, ret_decay_fwd, ret_decay_bwd, ret_norm_w, na_rpb, diff_lambda_q1, diff_lambda_k1, diff_lambda_q2, diff_lambda_k2, diff_norm_w, w_out, norm2_w, ffn_w_up, ffn_conv_w, ffn_conv_b, ffn_w_down, final_norm_w):
    layers = [_prepare_layer(l, norm1_w, ret_decay_fwd, ret_decay_bwd, ret_norm_w, na_rpb,
                             diff_lambda_q1, diff_lambda_k1, diff_lambda_q2, diff_lambda_k2,
                             diff_norm_w, norm2_w, ffn_conv_w, ffn_conv_b)
              for l in range(w_in.shape[0])]
    weights = _prepare_weights(w_in, w_out, ffn_w_up, ffn_w_down)
    slopes = 2.0 ** (-8.0 * (jnp.arange(DIFF_HEADS, dtype=f32) + 1.0) / DIFF_HEADS)
    slope_vecs = diff_slope_vectors(slopes)
    y_prompt = _trunk(x_prompt, layers, weights, final_norm_w, slope_vecs)
    y_sample = _trunk(x_sample, layers, weights, final_norm_w, slope_vecs)
    return (y_prompt, y_sample)
```
